```python
import math
import jax
import jax.numpy as jnp
from jax import lax
import numpy as np

D_MODEL = 1024
BATCH = 4
SEQ = 4096
DEPTH = 1

MEM_LEN = 256
EPS = 1e-6

SSD_EXPAND = 2
SSD_D_INNER = SSD_EXPAND * D_MODEL
SSD_HEAD_DIM = 64
SSD_HEADS = SSD_D_INNER // SSD_HEAD_DIM
SSD_GROUPS = 8
SSD_HEADS_PER_GROUP = SSD_HEADS // SSD_GROUPS
SSD_STATE = 128
SSD_CONV = 4
SSD_CHUNK = 128
SSD_CONV_DIM = SSD_D_INNER + 2 * SSD_GROUPS * SSD_STATE

DIL_PATTERNS = ((128, 1), (512, 4), (2048, 16))
DIL_GROUPS = len(DIL_PATTERNS)
DIL_HEADS_PER_GROUP = 8
DIL_HEAD_DIM = 64
DIL_WIDTH = DIL_GROUPS * DIL_HEADS_PER_GROUP * DIL_HEAD_DIM
DIL_OUT_WIDTH = DIL_HEADS_PER_GROUP * DIL_HEAD_DIM
DIL_BLOCK = 128
ROPE_THETA = 10000.0

MEM_HEADS = 4
MEM_HEAD_DIM = 384
MEM_WIDTH = MEM_HEADS * MEM_HEAD_DIM

N_BRANCHES = 3
IN_WIDTHS = (SSD_D_INNER, SSD_CONV_DIM, SSD_HEADS, DIL_WIDTH, DIL_WIDTH, DIL_WIDTH, MEM_WIDTH, N_BRANCHES * D_MODEL)
D_IN_PROJ = sum(IN_WIDTHS)

PEER_N_KEYS = 128
PEER_N_EXPERTS = PEER_N_KEYS * PEER_N_KEYS
PEER_HEADS = 8
PEER_TOPK = 16
PEER_QUERY_DIM = 256
PEER_HALF = PEER_QUERY_DIM // 2
PEER_BLOCK = 128

kernel_name = 'hybrid_ssd_dilated_memory_peer_block'


def _offsets(widths):
    out, acc = [], 0
    for w in widths[:-1]:
        acc += w
        out.append(acc)
    return out


def rmsnorm(x, gain):
    xf = x.astype(jnp.float32)
    y = xf * lax.rsqrt(jnp.mean(xf * xf, axis=-1, keepdims=True) + EPS)
    return (y * gain.astype(jnp.float32)).astype(x.dtype)


def rope(t, positions):
    half = t.shape[-1] // 2
    inv_freq = ROPE_THETA ** (-jnp.arange(half, dtype=jnp.float32) / half)
    ang = positions.astype(jnp.float32)[:, :, None, None] * inv_freq
    cos, sin = jnp.cos(ang), jnp.sin(ang)
    tf = t.astype(jnp.float32)
    t1, t2 = tf[..., :half], tf[..., half:]
    return jnp.concatenate([t1 * cos - t2 * sin, t2 * cos + t1 * sin], axis=-1).astype(t.dtype)


def causal_depthwise_conv(x, w, b):
    k_width, channels = w.shape
    y = lax.conv_general_dilated(x, w[:, None, :].astype(x.dtype), window_strides=(1,),
                                 padding=[(k_width - 1, 0)],
                                 dimension_numbers=('NWC', 'WIO', 'NWC'),
                                 feature_group_count=channels)
    return y + b.astype(x.dtype)


def segsum(a):
    t = a.shape[-1]
    cs = jnp.cumsum(a, axis=-1)
    seg = cs[..., :, None] - cs[..., None, :]
    return jnp.where(jnp.tril(jnp.ones((t, t), dtype=bool)), seg, -jnp.inf)


def ssd_chunked_scan(x, a, b, c):
    bsz, seq, g, r, p = x.shape
    n = b.shape[-1]
    nc = seq // SSD_CHUNK
    x = x.reshape(bsz, nc, SSD_CHUNK, g, r, p)
    b = b.reshape(bsz, nc, SSD_CHUNK, g, n)
    c = c.reshape(bsz, nc, SSD_CHUNK, g, n)
    a = a.reshape(bsz, nc, SSD_CHUNK, g, r).transpose(0, 3, 4, 1, 2)
    a_cs = jnp.cumsum(a, axis=-1)
    decay = jnp.exp(segsum(a)).astype(x.dtype)
    cb = jnp.einsum('bclgn,bcsgn->bcgls', c, b)
    y_diag = jnp.einsum('bcgls,bgrcls,bcsgrp->bclgrp', cb, decay, x)
    decay_states = jnp.exp(a_cs[..., -1:] - a_cs).astype(x.dtype)
    states = jnp.einsum('bclgn,bgrcl,bclgrp->bcgrpn', b, decay_states, x)
    states = jnp.concatenate([jnp.zeros_like(states[:, :1]), states], axis=1)
    chunk_ends = jnp.pad(a_cs[..., -1], ((0, 0), (0, 0), (0, 0), (1, 0)))
    chunk_decay = jnp.exp(segsum(chunk_ends)).astype(x.dtype)
    states = jnp.einsum('bgrzc,bcgrpn->bzgrpn', chunk_decay, states)[:, :-1]
    y_off = jnp.einsum('bclgn,bcgrpn,bgrcl->bclgrp', c, states, jnp.exp(a_cs).astype(x.dtype))
    return (y_diag + y_off).reshape(bsz, seq, g, r, p)


def ssd_mixer(z, xbc, dt, conv_w, conv_b, dt_bias, a_log, d_skip, ssd_norm):
    bsz, seq, _ = z.shape
    g, r, p, n = SSD_GROUPS, SSD_HEADS_PER_GROUP, SSD_HEAD_DIM, SSD_STATE
    xbc = jax.nn.silu(causal_depthwise_conv(xbc, conv_w, conv_b))
    xs, bs, cs = jnp.split(xbc, [SSD_D_INNER, SSD_D_INNER + g * n], axis=-1)
    xs = xs.reshape(bsz, seq, g, r, p)
    bs = bs.reshape(bsz, seq, g, n)
    cs = cs.reshape(bsz, seq, g, n)
    dt = jax.nn.softplus(dt.astype(jnp.float32) + dt_bias.astype(jnp.float32)).reshape(bsz, seq, g, r)
    a = -jnp.exp(a_log.astype(jnp.float32)).reshape(g, r)
    y = ssd_chunked_scan(xs * dt[..., None].astype(xs.dtype), a * dt, bs, cs)
    y = y + d_skip.reshape(g, r)[:, :, None].astype(xs.dtype) * xs
    y = y.reshape(bsz, seq, SSD_D_INNER)
    return rmsnorm(y * jax.nn.silu(z), ssd_norm)


def dilated_group(q, k, v, window, dilation):
    bsz, seq, h, dh = q.shape
    steps = window // dilation
    s_len = seq // dilation
    nb = -(-s_len // DIL_BLOCK)
    s_pad = nb * DIL_BLOCK

    def to_stream(t):
        return t.reshape(bsz, s_len, dilation, h, dh).transpose(0, 2, 3, 1, 4)

    qs, ks, vs = to_stream(q), to_stream(k), to_stream(v)
    qs = jnp.pad(qs, ((0, 0), (0, 0), (0, 0), (0, s_pad - s_len), (0, 0))).reshape(bsz, dilation, h, nb, DIL_BLOCK, dh)
    kv_pad = ((0, 0), (0, 0), (0, 0), (DIL_BLOCK, s_pad - s_len), (0, 0))

    def band(t):
        t = jnp.pad(t, kv_pad)
        prev = t[:, :, :, :s_pad].reshape(bsz, dilation, h, nb, DIL_BLOCK, dh)
        cur = t[:, :, :, DIL_BLOCK:].reshape(bsz, dilation, h, nb, DIL_BLOCK, dh)
        return jnp.concatenate([prev, cur], axis=4)

    kb, vb = band(ks), band(vs)
    s = jnp.einsum('brhnqd,brhnkd->brhnqk', qs, kb, preferred_element_type=jnp.float32) / math.sqrt(dh)
    qi = jnp.arange(nb)[:, None] * DIL_BLOCK + jnp.arange(DIL_BLOCK)[None, :]
    kj = jnp.arange(nb)[:, None] * DIL_BLOCK - DIL_BLOCK + jnp.arange(2 * DIL_BLOCK)[None, :]
    dist = qi[:, :, None] - kj[:, None, :]
    mask = (dist >= 0) & (dist <= steps) & (kj[:, None, :] >= 0)
    s = jnp.where(mask, s, -jnp.inf)
    m = jnp.max(s, axis=-1, keepdims=True)
    pexp = jnp.exp(s - m)
    den = jnp.sum(pexp, axis=-1, keepdims=True)
    o = jnp.einsum('brhnqk,brhnkd->brhnqd', (pexp / den).astype(v.dtype), vb)
    lse = (m + jnp.log(den))[..., 0]
    o = o.reshape(bsz, dilation, h, s_pad, dh)[:, :, :, :s_len].transpose(0, 3, 1, 2, 4).reshape(bsz, seq, h, dh)
    lse = lse.reshape(bsz, dilation, h, s_pad)[..., :s_len].transpose(0, 3, 1, 2).reshape(bsz, seq, h)
    return o, lse


def dilated_attention(q, k, v, positions, q_gain, k_gain):
    bsz, seq, _ = q.shape
    h_all = DIL_GROUPS * DIL_HEADS_PER_GROUP
    q = rope(rmsnorm(q.reshape(bsz, seq, h_all, DIL_HEAD_DIM), q_gain), positions)
    k = rope(rmsnorm(k.reshape(bsz, seq, h_all, DIL_HEAD_DIM), k_gain), positions)
    v = v.reshape(bsz, seq, h_all, DIL_HEAD_DIM)
    outs, lses = [], []
    for g, (window, dilation) in enumerate(DIL_PATTERNS):
        sl = slice(g * DIL_HEADS_PER_GROUP, (g + 1) * DIL_HEADS_PER_GROUP)
        o, lse = dilated_group(q[:, :, sl], k[:, :, sl], v[:, :, sl], window, dilation)
        outs.append(o)
        lses.append(lse)
    o = jnp.stack(outs, axis=2)
    w = jax.nn.softmax(jnp.stack(lses, axis=2), axis=2)
    return jnp.einsum('blgh,blghd->blhd', w.astype(o.dtype), o).reshape(bsz, seq, DIL_OUT_WIDTH)


def memory_attention(q, mem, mem_norm, w_mem_kv, q_gain, k_gain):
    bsz, seq, _ = q.shape
    kv = jnp.einsum('bmd,de->bme', rmsnorm(mem, mem_norm), w_mem_kv)
    k, v = jnp.split(kv, 2, axis=-1)
    q = rmsnorm(q.reshape(bsz, seq, MEM_HEADS, MEM_HEAD_DIM), q_gain)
    k = rmsnorm(k.reshape(bsz, -1, MEM_HEADS, MEM_HEAD_DIM), k_gain)
    v = v.reshape(bsz, -1, MEM_HEADS, MEM_HEAD_DIM)
    s = jnp.einsum('blhd,bmhd->bhlm', q, k, preferred_element_type=jnp.float32) / math.sqrt(MEM_HEAD_DIM)
    p = jax.nn.softmax(s, axis=-1).astype(v.dtype)
    return jnp.einsum('bhlm,bmhd->blhd', p, v).reshape(bsz, seq, MEM_WIDTH)


def mixer_sublayer(x, mem, positions, norm_mix, w_in, conv_w, conv_b, dt_bias, a_log, d_skip, ssd_norm,
                   dil_q_norm, dil_k_norm, mem_norm, w_mem_kv, mem_q_norm, mem_k_norm,
                   w_up_ssd, w_up_dil, w_up_mem, w_out):
    bsz, seq, _ = x.shape
    h = rmsnorm(x, norm_mix)
    proj = jnp.einsum('bld,de->ble', h, w_in)
    z, xbc, dt, q_d, k_d, v_d, q_m, gate_logits = jnp.split(proj, _offsets(IN_WIDTHS), axis=-1)
    y_ssd = ssd_mixer(z, xbc, dt, conv_w, conv_b, dt_bias, a_log, d_skip, ssd_norm)
    y_dil = dilated_attention(q_d, k_d, v_d, positions, dil_q_norm, dil_k_norm)
    y_mem = memory_attention(q_m, mem, mem_norm, w_mem_kv, mem_q_norm, mem_k_norm)
    gates = jax.nn.sigmoid(gate_logits.astype(jnp.float32)).astype(x.dtype).reshape(bsz, seq, N_BRANCHES, D_MODEL)
    merged = (gates[:, :, 0] * (y_ssd @ w_up_ssd)
              + gates[:, :, 1] * (y_dil @ w_up_dil)
              + gates[:, :, 2] * (y_mem @ w_up_mem))
    return merged @ w_out


def peer_ffn(h, w_q, keys1, keys2, u_table, v_table):
    bsz, seq, d = h.shape
    t = bsz * seq
    ht = h.reshape(t, d)
    q = (ht @ w_q).reshape(t, PEER_HEADS, 2, PEER_HALF)
    s1 = jnp.einsum('thd,kd->thk', q[:, :, 0], keys1, preferred_element_type=jnp.float32)
    s2 = jnp.einsum('thd,kd->thk', q[:, :, 1], keys2, preferred_element_type=jnp.float32)
    v1, i1 = lax.top_k(s1, PEER_TOPK)
    v2, i2 = lax.top_k(s2, PEER_TOPK)
    cand = (v1[..., :, None] + v2[..., None, :]).reshape(t, PEER_HEADS, PEER_TOPK * PEER_TOPK)
    sc, ci = lax.top_k(cand, PEER_TOPK)
    e1 = jnp.take_along_axis(i1, ci // PEER_TOPK, axis=-1)
    e2 = jnp.take_along_axis(i2, ci % PEER_TOPK, axis=-1)
    expert = e1 * PEER_N_KEYS + e2
    gate = jax.nn.softmax(sc, axis=-1)
    nblk = t // PEER_BLOCK

    def block(args):
        hb, eb, gb = args
        act = jax.nn.gelu(jnp.einsum('thkd,td->thk', u_table[eb], hb), approximate=False)
        return jnp.einsum('thk,thkd->td', gb.astype(hb.dtype) * act, v_table[eb])

    out = lax.map(block, (ht.reshape(nblk, PEER_BLOCK, d),
                          expert.reshape(nblk, PEER_BLOCK, PEER_HEADS, PEER_TOPK),
                          gate.reshape(nblk, PEER_BLOCK, PEER_HEADS, PEER_TOPK)))
    return out.reshape(bsz, seq, d)


def setup_inputs(seed: int = 0) -> dict:
    key = jax.random.key(seed)
    ks = jax.random.split(key, 32)
    f32 = jnp.float32

    def nrm(k, shape, scale):
        return jax.random.normal(k, shape, f32) * scale

    def gain(k, n):
        return 1.0 + 0.02 * jax.random.normal(k, (DEPTH, n), f32)

    x = nrm(ks[0], (BATCH, SEQ, D_MODEL), 1.0)
    mem = nrm(ks[1], (BATCH, MEM_LEN, D_MODEL), 1.0)
    positions = (jax.random.randint(ks[2], (BATCH, 1), 0, 1024, dtype=jnp.int32)
                 + jnp.arange(SEQ, dtype=jnp.int32)[None, :])
    dt_init = jnp.exp(jax.random.uniform(ks[6], (DEPTH, SSD_HEADS), f32, math.log(1e-3), math.log(1e-1)))
    dt_bias = dt_init + jnp.log(-jnp.expm1(-dt_init))
    a_log = jnp.log(jax.random.uniform(ks[7], (DEPTH, SSD_HEADS), f32, 1.0, 16.0))
    return {
        'x': x,
        'mem': mem,
        'positions': positions,
        'norm_mix': gain(ks[3], D_MODEL),
        'w_in': nrm(ks[4], (DEPTH, D_MODEL, D_IN_PROJ), D_MODEL ** -0.5),
        'conv_w': nrm(ks[5], (DEPTH, SSD_CONV, SSD_CONV_DIM), SSD_CONV ** -0.5),
        'conv_b': nrm(ks[8], (DEPTH, SSD_CONV_DIM), 0.02),
        'dt_bias': dt_bias,
        'a_log': a_log,
        'd_skip': 1.0 + 0.02 * jax.random.normal(ks[9], (DEPTH, SSD_HEADS), f32),
        'ssd_norm': gain(ks[10], SSD_D_INNER),
        'dil_q_norm': gain(ks[11], DIL_HEAD_DIM),
        'dil_k_norm': gain(ks[12], DIL_HEAD_DIM),
        'mem_norm': gain(ks[13], D_MODEL),
        'w_mem_kv': nrm(ks[14], (DEPTH, D_MODEL, 2 * MEM_WIDTH), D_MODEL ** -0.5),
        'mem_q_norm': gain(ks[15], MEM_HEAD_DIM),
        'mem_k_norm': gain(ks[16], MEM_HEAD_DIM),
        'w_up_ssd': nrm(ks[17], (DEPTH, SSD_D_INNER, D_MODEL), SSD_D_INNER ** -0.5),
        'w_up_dil': nrm(ks[18], (DEPTH, DIL_OUT_WIDTH, D_MODEL), DIL_OUT_WIDTH ** -0.5),
        'w_up_mem': nrm(ks[19], (DEPTH, MEM_WIDTH, D_MODEL), MEM_WIDTH ** -0.5),
        'w_out': nrm(ks[20], (DEPTH, D_MODEL, D_MODEL), D_MODEL ** -0.5),
        'norm_ffn': gain(ks[21], D_MODEL),
        'peer_w_q': nrm(ks[22], (DEPTH, D_MODEL, PEER_HEADS * PEER_QUERY_DIM), D_MODEL ** -0.5),
        'peer_keys1': nrm(ks[23], (DEPTH, PEER_N_KEYS, PEER_HALF), PEER_HALF ** -0.5),
        'peer_keys2': nrm(ks[24], (DEPTH, PEER_N_KEYS, PEER_HALF), PEER_HALF ** -0.5),
        'peer_u': nrm(ks[25], (DEPTH, PEER_N_EXPERTS, D_MODEL), D_MODEL ** -0.5),
        'peer_v': nrm(ks[26], (DEPTH, PEER_N_EXPERTS, D_MODEL), PEER_HEADS ** -0.5),
    }


def reference(x, mem, positions, norm_mix, w_in, conv_w, conv_b, dt_bias, a_log, d_skip, ssd_norm,
              dil_q_norm, dil_k_norm, mem_norm, w_mem_kv, mem_q_norm, mem_k_norm,
              w_up_ssd, w_up_dil, w_up_mem, w_out, norm_ffn, peer_w_q, peer_keys1, peer_keys2,
              peer_u, peer_v):
    for layer in range(DEPTH):
        x = x + mixer_sublayer(x, mem, positions, norm_mix[layer], w_in[layer], conv_w[layer], conv_b[layer],
                               dt_bias[layer], a_log[layer], d_skip[layer], ssd_norm[layer],
                               dil_q_norm[layer], dil_k_norm[layer], mem_norm[layer], w_mem_kv[layer],
                               mem_q_norm[layer], mem_k_norm[layer], w_up_ssd[layer], w_up_dil[layer],
                               w_up_mem[layer], w_out[layer])
        x = x + peer_ffn(rmsnorm(x, norm_ffn[layer]), peer_w_q[layer], peer_keys1[layer], peer_keys2[layer],
                         peer_u[layer], peer_v[layer])
    return x
```

```python
import functools
import math

import jax
import jax.numpy as jnp
from jax import lax
from jax.experimental import pallas as pl
from jax.experimental.pallas import tpu as pltpu

F32 = jnp.float32
BF16 = jnp.bfloat16
EPS = 1e-6
NEG = -1e30

VMEM_LIMIT_BYTES = 56 * 1024 * 1024
LANES = 128

SSD_CHUNK = 128
SSD_HEAD_DIM = 64
SSD_STATE = 128
SSD_GROUPS = 8
SSD_HEADS_PER_GROUP = 4
SSD_GROUP_WIDTH = SSD_HEADS_PER_GROUP * SSD_HEAD_DIM
SSD_CONV = 4

DIL_DILATIONS = (1, 4, 16)
DIL_BLOCK = 128
DIL_SUPER = DIL_BLOCK * 16
DIL_HEAD_DIM = 64
DIL_GROUP_WIDTH = 8 * DIL_HEAD_DIM
ROPE_THETA = 10000.0

MEM_HEADS = 4
MEM_HEAD_DIM = 384

PEER_KEYS = 128
PEER_HEADS = 8
PEER_TOPK = 16
PEER_HALF = 128


def _cparams(*sem):
    return pltpu.CompilerParams(dimension_semantics=sem, vmem_limit_bytes=VMEM_LIMIT_BYTES)


def _dot(a, b):
    return jnp.dot(a, b, preferred_element_type=F32)


def _dot_nt(a, b):
    return lax.dot_general(a, b, (((1,), (1,)), ((), ())), preferred_element_type=F32)


def _split2(x):
    hi = x.astype(BF16)
    lo = (x - hi.astype(F32)).astype(BF16)
    return hi, lo


def _split3(x):
    hi = x.astype(BF16)
    r = x - hi.astype(F32)
    mid = r.astype(BF16)
    lo = (r - mid.astype(F32)).astype(BF16)
    return hi, mid, lo


def _dot_f32(a, b):
    ah, al = _split2(a)
    bh, bl = _split2(b)
    return _dot(ah, bh) + (_dot(ah, bl) + _dot(al, bh))


def _dot_sel_right(m, e):
    hi, mid, lo = _split3(m)
    return _dot(hi, e) + (_dot(mid, e) + _dot(lo, e))


def _dot_sel_left(e, m):
    hi, mid, lo = _split3(m)
    return _dot(e, hi) + (_dot(e, mid) + _dot(e, lo))


def _sigmoid(x):
    return 1.0 / (1.0 + jnp.exp(-x))


def _silu(x):
    return x * _sigmoid(x)


def _in_proj_kernel(x_ref, g_ref, w_ref, wdt_ref, o_ref, dt_ref, h_ref):
    @pl.when(pl.program_id(1) == 0)
    def _():
        x = x_ref[...]
        ms = jnp.mean(x * x, axis=-1, keepdims=True)
        h = x * lax.rsqrt(ms + EPS) * g_ref[...]
        h_ref[...] = h.astype(BF16)
        dt_ref[...] = _dot_f32(h, wdt_ref[...])

    o_ref[...] = _dot(h_ref[...], w_ref[...]).astype(o_ref.dtype)


def _in_proj(x2d, gain, w_main, w_dt, tm=1024, tn=1024):
    t, d = x2d.shape
    n = w_main.shape[1]
    return pl.pallas_call(
        _in_proj_kernel,
        grid=(t // tm, n // tn),
        in_specs=[
            pl.BlockSpec((tm, d), lambda i, j: (i, 0)),
            pl.BlockSpec((1, d), lambda i, j: (0, 0)),
            pl.BlockSpec((d, tn), lambda i, j: (0, j)),
            pl.BlockSpec((d, LANES), lambda i, j: (0, 0)),
        ],
        out_specs=[
            pl.BlockSpec((tm, tn), lambda i, j: (i, j)),
            pl.BlockSpec((tm, LANES), lambda i, j: (i, 0)),
        ],
        out_shape=[jax.ShapeDtypeStruct((t, n), BF16), jax.ShapeDtypeStruct((t, LANES), F32)],
        scratch_shapes=[pltpu.VMEM((tm, d), BF16)],
        compiler_params=_cparams("parallel", "arbitrary"),
        name="in_proj",
    )(x2d, gain, w_main, w_dt)


def _ssd_kernel(xbc_ref, z_ref, dt_ref, cw_ref, cb_ref, dtb_ref, alog_ref, dskip_ref, gain_ref, e_ref,
                o_ref, xh_ref, state_ref, y_ref):
    q = SSD_CHUNK
    d_inner = SSD_GROUPS * SSD_GROUP_WIDTH
    gn = SSD_GROUPS * SSD_STATE

    @pl.when(pl.program_id(1) == 0)
    def _():
        xh_ref[0:8, :] = jnp.zeros((8, xh_ref.shape[1]), F32)
        state_ref[...] = jnp.zeros(state_ref.shape, F32)

    xc = xbc_ref[...].astype(F32)
    xh_ref[8:8 + q, :] = xc
    cw = cw_ref[...]
    acc = cb_ref[...] + cw[3:4, :] * xc
    acc = acc + cw[2:3, :] * xh_ref[7:7 + q, :]
    acc = acc + cw[1:2, :] * xh_ref[6:6 + q, :]
    acc = acc + cw[0:1, :] * xh_ref[5:5 + q, :]
    xh_ref[0:8, :] = xc[q - 8:q, :]
    u = _silu(acc)

    pre = dt_ref[...] + dtb_ref[...]
    dt = jnp.maximum(pre, 0.0) + jnp.log1p(jnp.exp(-jnp.abs(pre)))
    adt = dt * (-jnp.exp(alog_ref[...]))
    row = lax.broadcasted_iota(jnp.int32, (q, q), 0)
    col = lax.broadcasted_iota(jnp.int32, (q, q), 1)
    causal = row >= col
    tril = jnp.where(causal, 1.0, 0.0).astype(BF16)
    a_cs = _dot_sel_left(tril, adt)
    a_cs_t = a_cs.T
    cs_end = a_cs[q - 1:q, :]
    stacked = jnp.concatenate([dt, jnp.exp(a_cs), jnp.exp(cs_end - a_cs)], axis=0)
    expanded = _dot_sel_right(stacked, e_ref[...])
    dt_x = expanded[0:q]
    ecs_x = expanded[q:2 * q]
    dstate_x = expanded[2 * q:3 * q]
    chunk_decay_x = ecs_x[q - 1:q, :]

    for g in range(SSD_GROUPS):
        c0 = g * SSD_GROUP_WIDTH
        xs_g = u[:, c0:c0 + SSD_GROUP_WIDTH]
        b_g = u[:, d_inner + g * SSD_STATE:d_inner + (g + 1) * SSD_STATE]
        c_g = u[:, d_inner + gn + g * SSD_STATE:d_inner + gn + (g + 1) * SSD_STATE].astype(BF16)
        b_gt = b_g.T.astype(BF16)
        cb = _dot(c_g, b_gt)
        xd = xs_g * dt_x[:, c0:c0 + SSD_GROUP_WIDTH]
        xd_b = xd.astype(BF16)
        xs_b = (xd * dstate_x[:, c0:c0 + SSD_GROUP_WIDTH]).astype(BF16)
        new_state = _dot(b_gt, xs_b)
        s_in = state_ref[g]
        y_off = _dot(c_g, s_in.astype(BF16)) * ecs_x[:, c0:c0 + SSD_GROUP_WIDTH]
        state_ref[g] = s_in * chunk_decay_x[:, c0:c0 + SSD_GROUP_WIDTH] + new_state
        y_diag = []
        for r in range(SSD_HEADS_PER_GROUP):
            hd = g * SSD_HEADS_PER_GROUP + r
            seg = a_cs[:, hd:hd + 1] - a_cs_t[hd:hd + 1, :]
            decay = jnp.where(causal, jnp.exp(jnp.where(causal, seg, 0.0)), 0.0)
            m = (cb * decay).astype(BF16)
            y_diag.append(_dot(m, xd_b[:, r * SSD_HEAD_DIM:(r + 1) * SSD_HEAD_DIM]))
        y_g = jnp.concatenate(y_diag, axis=1) + y_off + dskip_ref[:, c0:c0 + SSD_GROUP_WIDTH] * xs_g
        y_ref[:, c0:c0 + SSD_GROUP_WIDTH] = y_g

    yz = y_ref[...] * _silu(z_ref[...].astype(F32))
    ms = jnp.mean(yz * yz, axis=-1, keepdims=True)
    o_ref[...] = (yz * lax.rsqrt(ms + EPS) * gain_ref[...]).astype(o_ref.dtype)


def _ssd(proj, dt_raw, conv_w, conv_b, dt_bias, a_log, d_skip_x, gain, expand, bsz, seq):
    q = SSD_CHUNK
    nc = seq // q
    d_inner = SSD_GROUPS * SSD_GROUP_WIDTH
    conv_dim = conv_w.shape[1]
    const = lambda b, c: (0, 0)
    return pl.pallas_call(
        _ssd_kernel,
        grid=(bsz, nc),
        in_specs=[
            pl.BlockSpec((q, conv_dim), lambda b, c: (b * nc + c, 0)),
            pl.BlockSpec((q, d_inner), lambda b, c: (b * nc + c, conv_dim // d_inner)),
            pl.BlockSpec((q, LANES), lambda b, c: (b * nc + c, 0)),
            pl.BlockSpec((SSD_CONV, conv_dim), const),
            pl.BlockSpec((1, conv_dim), const),
            pl.BlockSpec((1, LANES), const),
            pl.BlockSpec((1, LANES), const),
            pl.BlockSpec((1, d_inner), const),
            pl.BlockSpec((1, d_inner), const),
            pl.BlockSpec((LANES, d_inner), const),
        ],
        out_specs=pl.BlockSpec((q, d_inner), lambda b, c: (b * nc + c, 0)),
        out_shape=jax.ShapeDtypeStruct((bsz * seq, d_inner), BF16),
        scratch_shapes=[
            pltpu.VMEM((8 + q, conv_dim), F32),
            pltpu.VMEM((SSD_GROUPS, SSD_STATE, SSD_GROUP_WIDTH), F32),
            pltpu.VMEM((q, d_inner), F32),
        ],
        compiler_params=_cparams("parallel", "arbitrary"),
        name="ssd",
    )(proj, proj, dt_raw, conv_w, conv_b, dt_bias, a_log, d_skip_x, gain, expand)


def _dil_prep_kernel(in_ref, pos_ref, qg_ref, kg_ref, invf_ref, bd_ref, o_ref, cos_ref, sin_ref, tmp_ref):
    kind = pl.program_id(1)
    g = pl.program_id(2)
    rows = in_ref.shape[0]

    @pl.when(jnp.logical_and(kind == 0, g == 0))
    def _():
        ang = pos_ref[...].astype(F32) * invf_ref[...]
        lane = lax.broadcasted_iota(jnp.int32, (1, LANES), 1)
        first_half = (lane % DIL_HEAD_DIM) < (DIL_HEAD_DIM // 2)
        cos_ref[...] = jnp.cos(ang)
        sin_ref[...] = jnp.sin(ang) * jnp.where(first_half, -1.0, 1.0)

    @pl.when(kind < 2)
    def _():
        gain = jnp.where(kind == 0, qg_ref[...], kg_ref[...])
        scale = jnp.where(kind == 0, 1.0 / math.sqrt(DIL_HEAD_DIM), 1.0)
        lane = lax.broadcasted_iota(jnp.int32, (1, LANES), 1)
        first_half = (lane % DIL_HEAD_DIM) < (DIL_HEAD_DIM // 2)
        for c in range(DIL_GROUP_WIDTH // LANES):
            xc = in_ref[:, c * LANES:(c + 1) * LANES].astype(F32)
            hi, lo = _split2(xc * xc)
            ss = _dot(hi, bd_ref[...]) + _dot(lo, bd_ref[...])
            xn = xc * lax.rsqrt(ss * (1.0 / DIL_HEAD_DIM) + EPS) * gain
            partner = jnp.where(first_half, pltpu.roll(xn, LANES - DIL_HEAD_DIM // 2, 1),
                                pltpu.roll(xn, DIL_HEAD_DIM // 2, 1))
            tmp_ref[c] = (xn * cos_ref[...] + partner * sin_ref[...]) * scale

    @pl.when(kind == 2)
    def _():
        for c in range(DIL_GROUP_WIDTH // LANES):
            tmp_ref[c] = in_ref[:, c * LANES:(c + 1) * LANES].astype(F32)

    @pl.when(g == 0)
    def _():
        for c in range(DIL_GROUP_WIDTH // LANES):
            o_ref[:, c * LANES:(c + 1) * LANES] = tmp_ref[c].astype(o_ref.dtype)

    for gi in (1, 2):
        d = DIL_DILATIONS[gi]

        @pl.when(g == gi)
        def _(d=d):
            for unit in range(rows // DIL_BLOCK):
                s, rho = unit // d, unit % d
                for c in range(DIL_GROUP_WIDTH // LANES):
                    src = tmp_ref[c, pl.ds(s * DIL_BLOCK * d + rho, DIL_BLOCK, stride=d), :]
                    o_ref[unit * DIL_BLOCK:(unit + 1) * DIL_BLOCK, c * LANES:(c + 1) * LANES] = src.astype(o_ref.dtype)


def _dil_prep(proj, pos2d, q_gain, k_gain, inv_freq, blockdiag, first_col_block):
    t = proj.shape[0]
    rows = DIL_SUPER
    w = DIL_GROUP_WIDTH
    const = lambda i, k, g: (0, 0)
    return pl.pallas_call(
        _dil_prep_kernel,
        grid=(t // rows, 3, 3),
        in_specs=[
            pl.BlockSpec((rows, w), lambda i, k, g: (i, first_col_block + k * 3 + g)),
            pl.BlockSpec((rows, 1), lambda i, k, g: (i, 0)),
            pl.BlockSpec((1, LANES), const),
            pl.BlockSpec((1, LANES), const),
            pl.BlockSpec((1, LANES), const),
            pl.BlockSpec((LANES, LANES), const),
        ],
        out_specs=pl.BlockSpec((rows, w), lambda i, k, g: (i, k * 3 + g)),
        out_shape=jax.ShapeDtypeStruct((t, 9 * w), BF16),
        scratch_shapes=[pltpu.VMEM((rows, LANES), F32), pltpu.VMEM((rows, LANES), F32),
                        pltpu.VMEM((w // LANES, rows, LANES), F32)],
        compiler_params=_cparams("parallel", "arbitrary", "arbitrary"),
        name="dil_prep",
    )(proj, pos2d, q_gain, k_gain, inv_freq, blockdiag)


def _dil_attn_kernel(q_ref, kc_ref, kp_ref, vc_ref, vp_ref, o_ref, kbuf, vbuf, acc_ref, m_ref, l_ref):
    n = pl.program_id(1)
    g = pl.program_id(2)
    rows = q_ref.shape[0]
    blk = DIL_BLOCK
    n_units = rows // blk

    kbuf[0:rows, :] = kp_ref[...]
    kbuf[rows:2 * rows, :] = kc_ref[...]
    vbuf[0:rows, :] = vp_ref[...]
    vbuf[rows:2 * rows, :] = vc_ref[...]

    qi = lax.broadcasted_iota(jnp.int32, (blk, blk), 0)
    kj = lax.broadcasted_iota(jnp.int32, (blk, blk), 1)
    mask_cur = kj <= qi
    mask_prev = kj >= qi

    def run_group(gi):
        d = DIL_DILATIONS[gi]

        def unit_body(u, carry):
            q_row = pl.multiple_of(u * blk, blk)
            c_row = pl.multiple_of(rows + u * blk, blk)
            p_row = pl.multiple_of(rows + u * blk - d * blk, blk)
            prev_bias = jnp.where(jnp.logical_or(n > 0, u >= d), 0.0, NEG)
            pos_row = (u // d) * (blk * d) + (u % d)
            pos_rows = pl.ds(pos_row, blk, stride=d) if d > 1 else pl.ds(pl.multiple_of(u * blk, blk), blk)
            for hp in range(DIL_GROUP_WIDTH // LANES):
                o_parts, m_parts, l_parts = [], [], []
                for hh in range(LANES // DIL_HEAD_DIM):
                    lanes = slice(hp * LANES + hh * DIL_HEAD_DIM, hp * LANES + (hh + 1) * DIL_HEAD_DIM)
                    qh = q_ref[pl.ds(q_row, blk), lanes]
                    s_cur = jnp.where(mask_cur, _dot_nt(qh, kbuf[pl.ds(c_row, blk), lanes]), NEG)
                    s_prev = jnp.where(mask_prev, _dot_nt(qh, kbuf[pl.ds(p_row, blk), lanes]), NEG) + prev_bias
                    m_u = jnp.maximum(jnp.max(s_cur, axis=1, keepdims=True), jnp.max(s_prev, axis=1, keepdims=True))
                    p_cur = jnp.exp(s_cur - m_u)
                    p_prev = jnp.exp(s_prev - m_u)
                    l_u = jnp.sum(p_cur, axis=1, keepdims=True) + jnp.sum(p_prev, axis=1, keepdims=True)
                    o_u = (_dot(p_cur.astype(BF16), vbuf[pl.ds(c_row, blk), lanes])
                           + _dot(p_prev.astype(BF16), vbuf[pl.ds(p_row, blk), lanes]))
                    o_parts.append(o_u)
                    m_parts.append(jnp.broadcast_to(m_u, (blk, DIL_HEAD_DIM)))
                    l_parts.append(jnp.broadcast_to(l_u, (blk, DIL_HEAD_DIM)))
                o_u = jnp.concatenate(o_parts, axis=1)
                m_u = jnp.concatenate(m_parts, axis=1)
                l_u = jnp.concatenate(l_parts, axis=1)
                if gi == 0:
                    acc_ref[hp, pos_rows, :] = o_u
                    m_ref[hp, pos_rows, :] = m_u
                    l_ref[hp, pos_rows, :] = l_u
                else:
                    m_old = m_ref[hp, pos_rows, :]
                    m_new = jnp.maximum(m_old, m_u)
                    a_old = jnp.exp(m_old - m_new)
                    a_new = jnp.exp(m_u - m_new)
                    acc_ref[hp, pos_rows, :] = acc_ref[hp, pos_rows, :] * a_old + o_u * a_new
                    l_ref[hp, pos_rows, :] = l_ref[hp, pos_rows, :] * a_old + l_u * a_new
                    m_ref[hp, pos_rows, :] = m_new
            return carry

        lax.fori_loop(0, n_units, unit_body, 0)

    for gi in range(len(DIL_DILATIONS)):
        pl.when(g == gi)(functools.partial(run_group, gi))

    @pl.when(g == len(DIL_DILATIONS) - 1)
    def _():
        for hp in range(DIL_GROUP_WIDTH // LANES):
            o_ref[:, hp * LANES:(hp + 1) * LANES] = (acc_ref[hp] / l_ref[hp]).astype(o_ref.dtype)


def _dil_attn(qkv, bsz, seq):
    rows = DIL_SUPER
    w = DIL_GROUP_WIDTH
    nsb = seq // rows
    cur = lambda off: (lambda b, n, g: (b * nsb + n, off + g))
    prev = lambda off: (lambda b, n, g: (b * nsb + jnp.maximum(n - 1, 0), off + g))
    return pl.pallas_call(
        _dil_attn_kernel,
        grid=(bsz, nsb, 3),
        in_specs=[
            pl.BlockSpec((rows, w), cur(0)),
            pl.BlockSpec((rows, w), cur(3)),
            pl.BlockSpec((rows, w), prev(3)),
            pl.BlockSpec((rows, w), cur(6)),
            pl.BlockSpec((rows, w), prev(6)),
        ],
        out_specs=pl.BlockSpec((rows, w), lambda b, n, g: (b * nsb + n, 0)),
        out_shape=jax.ShapeDtypeStruct((bsz * seq, w), BF16),
        scratch_shapes=[pltpu.VMEM((2 * rows, w), BF16), pltpu.VMEM((2 * rows, w), BF16),
                        pltpu.VMEM((w // LANES, rows, LANES), F32), pltpu.VMEM((w // LANES, rows, LANES), F32),
                        pltpu.VMEM((w // LANES, rows, LANES), F32)],
        compiler_params=_cparams("parallel", "arbitrary", "arbitrary"),
        name="dil_attn",
    )(qkv, qkv, qkv, qkv, qkv)


def _mem_kv_kernel(mem_ref, g_ref, w_ref, kg_ref, k_ref, v_ref):
    x = mem_ref[...]
    ms = jnp.mean(x * x, axis=-1, keepdims=True)
    h = (x * lax.rsqrt(ms + EPS) * g_ref[...]).astype(BF16)
    kv = _dot(h, w_ref[...])
    width = MEM_HEADS * MEM_HEAD_DIM
    for hd in range(MEM_HEADS):
        sl = slice(hd * MEM_HEAD_DIM, (hd + 1) * MEM_HEAD_DIM)
        kh = kv[:, sl]
        ms = jnp.mean(kh * kh, axis=-1, keepdims=True)
        k_ref[:, sl] = (kh * lax.rsqrt(ms + EPS) * kg_ref[...]).astype(k_ref.dtype)
    v_ref[...] = kv[:, width:].astype(v_ref.dtype)


def _mem_kv(mem2d, gain, w_kv, k_gain, bsz, mem_len):
    d = mem2d.shape[1]
    width = MEM_HEADS * MEM_HEAD_DIM
    const = lambda b: (0, 0)
    return pl.pallas_call(
        _mem_kv_kernel,
        grid=(bsz,),
        in_specs=[
            pl.BlockSpec((mem_len, d), lambda b: (b, 0)),
            pl.BlockSpec((1, d), const),
            pl.BlockSpec((d, 2 * width), const),
            pl.BlockSpec((1, MEM_HEAD_DIM), const),
        ],
        out_specs=[pl.BlockSpec((mem_len, width), lambda b: (b, 0)),
                   pl.BlockSpec((mem_len, width), lambda b: (b, 0))],
        out_shape=[jax.ShapeDtypeStruct((bsz * mem_len, width), BF16)] * 2,
        compiler_params=_cparams("parallel"),
        name="mem_kv",
    )(mem2d, gain, w_kv, k_gain)


def _mem_attn_kernel(q_ref, k_ref, v_ref, qg_ref, o_ref):
    for hd in range(MEM_HEADS):
        sl = slice(hd * MEM_HEAD_DIM, (hd + 1) * MEM_HEAD_DIM)
        qh = q_ref[:, sl].astype(F32)
        ms = jnp.mean(qh * qh, axis=-1, keepdims=True)
        qn = (qh * lax.rsqrt(ms + EPS) * qg_ref[...]).astype(BF16)
        s = _dot_nt(qn, k_ref[:, sl]) * (1.0 / math.sqrt(MEM_HEAD_DIM))
        p = jnp.exp(s - jnp.max(s, axis=-1, keepdims=True))
        den = jnp.sum(p, axis=-1, keepdims=True)
        o_ref[:, sl] = (_dot(p.astype(BF16), v_ref[:, sl]) / den).astype(o_ref.dtype)


def _mem_attn(proj, k, v, q_gain, bsz, seq, mem_len, q_col_block, tq=512):
    width = MEM_HEADS * MEM_HEAD_DIM
    nq = seq // tq
    return pl.pallas_call(
        _mem_attn_kernel,
        grid=(bsz, nq),
        in_specs=[
            pl.BlockSpec((tq, width), lambda b, i: (b * nq + i, q_col_block)),
            pl.BlockSpec((mem_len, width), lambda b, i: (b, 0)),
            pl.BlockSpec((mem_len, width), lambda b, i: (b, 0)),
            pl.BlockSpec((1, MEM_HEAD_DIM), lambda b, i: (0, 0)),
        ],
        out_specs=pl.BlockSpec((tq, width), lambda b, i: (b * nq + i, 0)),
        out_shape=jax.ShapeDtypeStruct((bsz * seq, width), BF16),
        compiler_params=_cparams("parallel", "parallel"),
        name="mem_attn",
    )(proj, k, v, q_gain)


def _combine_kernel(ys_ref, yd_ref, ym_ref, gl_ref, x_ref, ws_ref, wd_ref, wm_ref, wo_ref, o_ref):
    d = x_ref.shape[1]
    gates = _sigmoid(gl_ref[...].astype(F32))
    merged = gates[:, 0:d] * _dot(ys_ref[...], ws_ref[...])
    merged = merged + gates[:, d:2 * d] * _dot(yd_ref[...], wd_ref[...])
    merged = merged + gates[:, 2 * d:3 * d] * _dot(ym_ref[...], wm_ref[...])
    o_ref[...] = x_ref[...] + _dot(merged.astype(BF16), wo_ref[...])


def _combine(y_ssd, y_dil, y_mem, proj, x2d, w_ssd, w_dil, w_mem, w_out, gate_col_block, tm=512):
    t, d = x2d.shape
    const = lambda i: (0, 0)
    row = lambda i: (i, 0)
    return pl.pallas_call(
        _combine_kernel,
        grid=(t // tm,),
        in_specs=[
            pl.BlockSpec((tm, y_ssd.shape[1]), row),
            pl.BlockSpec((tm, y_dil.shape[1]), row),
            pl.BlockSpec((tm, y_mem.shape[1]), row),
            pl.BlockSpec((tm, 3 * d), lambda i: (i, gate_col_block)),
            pl.BlockSpec((tm, d), row),
            pl.BlockSpec(w_ssd.shape, const),
            pl.BlockSpec(w_dil.shape, const),
            pl.BlockSpec(w_mem.shape, const),
            pl.BlockSpec(w_out.shape, const),
        ],
        out_specs=pl.BlockSpec((tm, d), row),
        out_shape=jax.ShapeDtypeStruct((t, d), F32),
        compiler_params=_cparams("parallel"),
        name="combine",
    )(y_ssd, y_dil, y_mem, proj, x2d, w_ssd, w_dil, w_mem, w_out)


def _top_rows(work, out_ref, count):
    for k in range(count):
        m = jnp.max(work, axis=0, keepdims=True)
        out_ref[k:k + 1, :] = m
        work = jnp.where(work == m, -jnp.inf, work)


def _gelu(x):
    return 0.5 * x * (1.0 + lax.erf(x * (1.0 / math.sqrt(2.0))))


def _peer_kernel(x_ref, gain_ref, wq_ref, k1_ref, k2_ref, u_ref, vt_ref, o_ref,
                 hn_ref, s1_ref, s2_ref, p1_ref, p2_ref, tau_ref, top1_ref, top2_ref, cand_ref, sc_ref,
                 act_ref, w_ref, acc_ref):
    e = pl.program_id(1)
    n_e = pl.num_programs(1)
    tt = x_ref.shape[0]
    eb = u_ref.shape[0]
    k = PEER_TOPK
    nk = PEER_KEYS
    n_tc = tt // LANES
    n_i = eb // nk
    assert n_i == 8, "one aligned 8-row group of key-1 rows per expert block"

    @pl.when(e == 0)
    def _():
        x = x_ref[...]
        ms = jnp.mean(x * x, axis=-1, keepdims=True)
        hn = x * lax.rsqrt(ms + EPS) * gain_ref[...]
        hn_ref[...] = hn.T.astype(BF16)
        acc_ref[...] = jnp.zeros(acc_ref.shape, F32)
        for h in range(PEER_HEADS):
            r0 = h * 2 * PEER_HALF
            q1 = _dot(wq_ref[r0:r0 + PEER_HALF, :], hn_ref[...])
            q2 = _dot(wq_ref[r0 + PEER_HALF:r0 + 2 * PEER_HALF, :], hn_ref[...])
            s1 = _dot_f32(k1_ref[...], q1)
            s2 = _dot_f32(k2_ref[...], q2)
            _top_rows(s1, top1_ref, k)
            _top_rows(s2, top2_ref, k)
            v1 = top1_ref[...]
            v2 = top2_ref[...]
            brow = lax.broadcasted_iota(jnp.int32, (k, 1), 0)
            groups = []
            for a in range(k // 2):
                nb = k // (a + 1)
                rows_b = 8 if nb <= 8 else k
                blk = v1[a:a + 1, :] + v2[0:rows_b, :]
                groups.append(jnp.where(brow[0:rows_b] < nb, blk, -jnp.inf))
            groups.append(v1[k // 2:k, :] + v2[0:1, :])
            cand = jnp.concatenate(groups, axis=0)
            cand_ref[0:cand.shape[0], :] = cand
            _top_rows(cand_ref[0:cand.shape[0], :], sc_ref, k)
            sc = sc_ref[...]
            tau = jnp.broadcast_to(sc[k - 1:k, :], (8, tt))
            z = jnp.sum(jnp.exp(sc - sc[0:1, :]), axis=0, keepdims=True)
            p1 = jnp.exp(s1 - v1[0:1, :])
            p2 = jnp.exp(s2 - v2[0:1, :]) / z
            for tc in range(n_tc):
                ts = slice(tc * LANES, (tc + 1) * LANES)
                s1_ref[h, tc] = s1[:, ts]
                s2_ref[h, tc] = s2[:, ts]
                p1_ref[h, tc] = p1[:, ts]
                p2_ref[h, tc] = p2[:, ts]
                tau_ref[h, tc] = tau[:, ts]

    act = _gelu(_dot(u_ref[...], hn_ref[...]))
    for tc in range(n_tc):
        act_ref[tc] = act[:, tc * LANES:(tc + 1) * LANES]

    i0 = pl.multiple_of(e * n_i, n_i)

    def chunk_body(tc, carry):
        s1_rows = [s1_ref[h, tc, pl.ds(i0, n_i), :] for h in range(PEER_HEADS)]
        p1_rows = [p1_ref[h, tc, pl.ds(i0, n_i), :] for h in range(PEER_HEADS)]
        for ii in range(n_i):
            gsum = jnp.zeros((nk, LANES), F32)
            for h in range(PEER_HEADS):
                s = s1_rows[h][ii:ii + 1, :] + s2_ref[h, tc]
                pp = p1_rows[h][ii:ii + 1, :] * p2_ref[h, tc]
                gsum = gsum + jnp.where(s >= tau_ref[h, tc, 0:1, :], pp, 0.0)
            w_ref[tc, ii * nk:(ii + 1) * nk, :] = (gsum * act_ref[tc, ii * nk:(ii + 1) * nk, :]).astype(BF16)
        return carry

    lax.fori_loop(0, n_tc, chunk_body, 0)
    w = jnp.concatenate([w_ref[tc] for tc in range(n_tc)], axis=1)
    acc_ref[...] += _dot(vt_ref[...], w)

    @pl.when(e == n_e - 1)
    def _():
        o_ref[...] = x_ref[...] + acc_ref[...].T


def _peer(x2d, gain, wq_t, keys1, keys2, u_b, vt_b, tt=512, eb=1024):
    t, d = x2d.shape
    n_exp = u_b.shape[0]
    const = lambda i, e: (0, 0)
    n_cand = 8 * 10
    n_tc = tt // LANES
    return pl.pallas_call(
        _peer_kernel,
        grid=(t // tt, n_exp // eb),
        in_specs=[
            pl.BlockSpec((tt, d), lambda i, e: (i, 0)),
            pl.BlockSpec((1, d), const),
            pl.BlockSpec(wq_t.shape, const),
            pl.BlockSpec(keys1.shape, const),
            pl.BlockSpec(keys2.shape, const),
            pl.BlockSpec((eb, d), lambda i, e: (e, 0)),
            pl.BlockSpec((d, eb), lambda i, e: (0, e)),
        ],
        out_specs=pl.BlockSpec((tt, d), lambda i, e: (i, 0)),
        out_shape=jax.ShapeDtypeStruct((t, d), F32),
        scratch_shapes=[
            pltpu.VMEM((d, tt), BF16),
            pltpu.VMEM((PEER_HEADS, n_tc, PEER_KEYS, LANES), F32),
            pltpu.VMEM((PEER_HEADS, n_tc, PEER_KEYS, LANES), F32),
            pltpu.VMEM((PEER_HEADS, n_tc, PEER_KEYS, LANES), F32),
            pltpu.VMEM((PEER_HEADS, n_tc, PEER_KEYS, LANES), F32),
            pltpu.VMEM((PEER_HEADS, n_tc, 8, LANES), F32),
            pltpu.VMEM((PEER_TOPK, tt), F32),
            pltpu.VMEM((PEER_TOPK, tt), F32),
            pltpu.VMEM((n_cand, tt), F32),
            pltpu.VMEM((PEER_TOPK, tt), F32),
            pltpu.VMEM((n_tc, eb, LANES), F32),
            pltpu.VMEM((n_tc, eb, LANES), BF16),
            pltpu.VMEM((d, tt), F32),
        ],
        compiler_params=_cparams("parallel", "arbitrary"),
        name="peer",
    )(x2d, gain, wq_t, keys1, keys2, u_b, vt_b)


def _pad_lanes(v, fill=0.0):
    return jnp.pad(v, ((0, 0), (0, LANES - v.shape[1])), constant_values=fill)


def _layer(x2d, mem2d, pos2d, bsz, seq, mem_len, norm_mix, w_in, conv_w, conv_b, dt_bias, a_log, d_skip,
           ssd_norm, dil_q_norm, dil_k_norm, mem_norm, w_mem_kv, mem_q_norm, mem_k_norm,
           w_up_ssd, w_up_dil, w_up_mem, w_out, norm_ffn, peer_w_q, peer_keys1, peer_keys2, peer_u, peer_v):
    d_inner = SSD_GROUPS * SSD_GROUP_WIDTH
    conv_dim = d_inner + 2 * SSD_GROUPS * SSD_STATE
    n_heads = SSD_GROUPS * SSD_HEADS_PER_GROUP
    dil_w = 3 * DIL_GROUP_WIDTH
    mem_w = MEM_HEADS * MEM_HEAD_DIM
    d = x2d.shape[1]

    o = 0
    parts = {}
    for name, width in (("z", d_inner), ("xbc", conv_dim), ("dt", n_heads), ("q_d", dil_w), ("k_d", dil_w),
                        ("v_d", dil_w), ("q_m", mem_w), ("gates", 3 * d)):
        parts[name] = w_in[:, o:o + width]
        o += width
    w_main = jnp.concatenate([parts[k] for k in ("xbc", "z", "q_d", "k_d", "v_d", "q_m", "gates")],
                             axis=1).astype(BF16)
    w_dt = _pad_lanes(parts["dt"])
    z_off = conv_dim
    qd_off = z_off + d_inner
    qm_off = qd_off + 3 * dil_w
    gate_off = qm_off + mem_w

    proj, dt_raw = _in_proj(x2d, norm_mix[None, :], w_main, w_dt)

    heads = jnp.arange(LANES)[:, None]
    chans = jnp.arange(d_inner)[None, :]
    expand = (chans // SSD_HEAD_DIM == heads).astype(BF16)
    y_ssd = _ssd(proj, dt_raw, conv_w, conv_b[None, :], _pad_lanes(dt_bias[None, :]), _pad_lanes(a_log[None, :]),
                 jnp.repeat(d_skip, SSD_HEAD_DIM)[None, :], ssd_norm[None, :], expand, bsz, seq)

    half = DIL_HEAD_DIM // 2
    inv_freq = ROPE_THETA ** (-jnp.arange(half, dtype=F32) / half)
    inv_freq = jnp.tile(inv_freq, LANES // half)[None, :]
    lane = jnp.arange(LANES)
    blockdiag = (lane[:, None] // DIL_HEAD_DIM == lane[None, :] // DIL_HEAD_DIM).astype(BF16)
    qkv = _dil_prep(proj, pos2d, jnp.tile(dil_q_norm, LANES // DIL_HEAD_DIM)[None, :],
                    jnp.tile(dil_k_norm, LANES // DIL_HEAD_DIM)[None, :], inv_freq, blockdiag,
                    qd_off // DIL_GROUP_WIDTH)
    y_dil = _dil_attn(qkv, bsz, seq)

    k_mem, v_mem = _mem_kv(mem2d, mem_norm[None, :], w_mem_kv.astype(BF16), mem_k_norm[None, :], bsz, mem_len)
    y_mem = _mem_attn(proj, k_mem, v_mem, mem_q_norm[None, :], bsz, seq, mem_len, qm_off // mem_w)

    x_mid = _combine(y_ssd, y_dil, y_mem, proj, x2d, w_up_ssd.astype(BF16), w_up_dil.astype(BF16),
                     w_up_mem.astype(BF16), w_out.astype(BF16), gate_off // (3 * d))

    return _peer(x_mid, norm_ffn[None, :], peer_w_q.T.astype(BF16), peer_keys1, peer_keys2,
                 peer_u.astype(BF16), peer_v.T.astype(BF16))


def kernel(x, mem, positions, norm_mix, w_in, conv_w, conv_b, dt_bias, a_log, d_skip, ssd_norm, dil_q_norm,
           dil_k_norm, mem_norm, w_mem_kv, mem_q_norm, mem_k_norm, w_up_ssd, w_up_dil, w_up_mem, w_out,
           norm_ffn, peer_w_q, peer_keys1, peer_keys2, peer_u, peer_v):
    bsz, seq, d = x.shape
    mem_len = mem.shape[1]
    x2d = x.reshape(bsz * seq, d)
    mem2d = mem.reshape(bsz * mem_len, d)
    pos2d = positions.reshape(bsz * seq, 1)
    for layer in range(norm_mix.shape[0]):
        x2d = _layer(x2d, mem2d, pos2d, bsz, seq, mem_len, norm_mix[layer], w_in[layer], conv_w[layer],
                     conv_b[layer], dt_bias[layer], a_log[layer], d_skip[layer], ssd_norm[layer],
                     dil_q_norm[layer], dil_k_norm[layer], mem_norm[layer], w_mem_kv[layer], mem_q_norm[layer],
                     mem_k_norm[layer], w_up_ssd[layer], w_up_dil[layer], w_up_mem[layer], w_out[layer],
                     norm_ffn[layer], peer_w_q[layer], peer_keys1[layer], peer_keys2[layer], peer_u[layer],
                     peer_v[layer])
    return x2d.reshape(bsz, seq, d)
```

```python
import functools
import math

import jax
import jax.numpy as jnp
from jax import lax
from jax.experimental import pallas as pl
from jax.experimental.pallas import tpu as pltpu

F32 = jnp.float32
BF16 = jnp.bfloat16
EPS = 1e-6
NEG = -1e30

VMEM_LIMIT_BYTES = 56 * 1024 * 1024
LANES = 128

SSD_CHUNK = 128
SSD_HEAD_DIM = 64
SSD_STATE = 128
SSD_GROUPS = 8
SSD_HEADS_PER_GROUP = 4
SSD_GROUP_WIDTH = SSD_HEADS_PER_GROUP * SSD_HEAD_DIM
SSD_CONV = 4
SSD_HISTORY_ROWS = 16
SSD_CONV_COLS = 512

DIL_DILATIONS = (1, 4, 16)
DIL_BLOCK = 128
DIL_SUPER = DIL_BLOCK * 16
DIL_HEAD_DIM = 64
DIL_GROUP_WIDTH = 8 * DIL_HEAD_DIM
ROPE_THETA = 10000.0

MEM_HEADS = 4
MEM_HEAD_DIM = 384

PEER_KEYS = 128
PEER_HEADS = 8
PEER_TOPK = 16
PEER_HALF = 128


def _cparams(*sem):
    return pltpu.CompilerParams(dimension_semantics=sem, vmem_limit_bytes=VMEM_LIMIT_BYTES)


def _dot(a, b):
    return jnp.dot(a, b, preferred_element_type=F32)


def _dot_nt(a, b):
    return lax.dot_general(a, b, (((1,), (1,)), ((), ())), preferred_element_type=F32)


def _split2(x):
    hi = x.astype(BF16)
    lo = (x - hi.astype(F32)).astype(BF16)
    return hi, lo


def _split3(x):
    hi = x.astype(BF16)
    r = x - hi.astype(F32)
    mid = r.astype(BF16)
    lo = (r - mid.astype(F32)).astype(BF16)
    return hi, mid, lo


def _dot_f32(a, b):
    ah, al = _split2(a)
    bh, bl = _split2(b)
    return _dot(ah, bh) + (_dot(ah, bl) + _dot(al, bh))


def _dot_sel_right(m, e):
    hi, mid, lo = _split3(m)
    return _dot(hi, e) + (_dot(mid, e) + _dot(lo, e))


def _dot_sel_left(e, m):
    hi, mid, lo = _split3(m)
    return _dot(e, hi) + (_dot(e, mid) + _dot(e, lo))


def _sigmoid(x):
    return 1.0 / (1.0 + jnp.exp(-x))


def _silu(x):
    return x * _sigmoid(x)


def _in_proj_kernel(x_ref, g_ref, w_ref, wdt_ref, o_ref, dt_ref, h_ref):
    @pl.when(pl.program_id(1) == 0)
    def _():
        x = x_ref[...]
        ms = jnp.mean(x * x, axis=-1, keepdims=True)
        h = x * lax.rsqrt(ms + EPS) * g_ref[...]
        h_ref[...] = h.astype(BF16)
        dt_ref[...] = _dot_f32(h, wdt_ref[...])

    o_ref[...] = _dot(h_ref[...], w_ref[...]).astype(o_ref.dtype)


def _in_proj(x2d, gain, w_main, w_dt, tm=1024, tn=1024):
    t, d = x2d.shape
    n = w_main.shape[1]
    return pl.pallas_call(
        _in_proj_kernel,
        grid=(t // tm, n // tn),
        in_specs=[
            pl.BlockSpec((tm, d), lambda i, j: (i, 0)),
            pl.BlockSpec((1, d), lambda i, j: (0, 0)),
            pl.BlockSpec((d, tn), lambda i, j: (0, j)),
            pl.BlockSpec((d, LANES), lambda i, j: (0, 0)),
        ],
        out_specs=[
            pl.BlockSpec((tm, tn), lambda i, j: (i, j)),
            pl.BlockSpec((tm, LANES), lambda i, j: (i, 0)),
        ],
        out_shape=[jax.ShapeDtypeStruct((t, n), BF16), jax.ShapeDtypeStruct((t, LANES), F32)],
        scratch_shapes=[pltpu.VMEM((tm, d), BF16)],
        compiler_params=_cparams("parallel", "arbitrary"),
        name="in_proj",
    )(x2d, gain, w_main, w_dt)


def _ssd_kernel(xbc_ref, z_ref, dt_ref, cw_ref, cb_ref, dtb_ref, alog_ref, dskip_ref, gain_ref, e_ref, shift_ref,
                o_ref, xh_ref, u_ref, state_ref, y_ref):
    q = SSD_CHUNK
    d_inner = SSD_GROUPS * SSD_GROUP_WIDTH
    gn = SSD_GROUPS * SSD_STATE

    hist = SSD_HISTORY_ROWS

    @pl.when(pl.program_id(1) == 0)
    def _():
        xh_ref[0:hist, :] = jnp.zeros((hist, xh_ref.shape[1]), BF16)
        state_ref[...] = jnp.zeros(state_ref.shape, F32)

    xh_ref[hist:hist + q, :] = xbc_ref[...]
    for cblk in range(xh_ref.shape[1] // SSD_CONV_COLS):
        cs = slice(cblk * SSD_CONV_COLS, (cblk + 1) * SSD_CONV_COLS)
        delayed = _dot(shift_ref[...], xh_ref[:, cs])
        acc = cb_ref[:, cs] + cw_ref[3:4, cs] * xbc_ref[:, cs].astype(F32)
        for k in range(1, SSD_CONV):
            acc = acc + cw_ref[SSD_CONV - 1 - k:SSD_CONV - k, cs] * delayed[(k - 1) * q:k * q, :]
        u_ref[:, cs] = _silu(acc)
    xh_ref[0:hist, :] = xh_ref[q:q + hist, :]

    pre = dt_ref[...] + dtb_ref[...]
    dt = jnp.maximum(pre, 0.0) + jnp.log1p(jnp.exp(-jnp.abs(pre)))
    adt = dt * (-jnp.exp(alog_ref[...]))
    row = lax.broadcasted_iota(jnp.int32, (q, q), 0)
    col = lax.broadcasted_iota(jnp.int32, (q, q), 1)
    causal = row >= col
    tril = jnp.where(causal, 1.0, 0.0).astype(BF16)
    a_cs = _dot_sel_left(tril, adt)
    a_cs_t = a_cs.T
    dt_x = _dot(dt.astype(BF16), e_ref[...])
    acs_x = _dot_sel_right(a_cs, e_ref[...])
    ecs_x = jnp.exp(acs_x)
    dstate_x = jnp.exp(acs_x[q - 1:q, :] - acs_x)
    chunk_decay_x = ecs_x[q - 1:q, :]

    cbs, xds = [], []
    for g in range(SSD_GROUPS):
        c0 = g * SSD_GROUP_WIDTH
        xs_g = u_ref[:, c0:c0 + SSD_GROUP_WIDTH]
        b_g = u_ref[:, d_inner + g * SSD_STATE:d_inner + (g + 1) * SSD_STATE]
        c_g = u_ref[:, d_inner + gn + g * SSD_STATE:d_inner + gn + (g + 1) * SSD_STATE].astype(BF16)
        b_gt = b_g.T.astype(BF16)
        cbs.append(_dot(c_g, b_gt))
        xd = xs_g * dt_x[:, c0:c0 + SSD_GROUP_WIDTH]
        xds.append(xd.astype(BF16))
        xs_b = (xd * dstate_x[:, c0:c0 + SSD_GROUP_WIDTH]).astype(BF16)
        new_state = _dot(b_gt, xs_b)
        s_in = state_ref[g]
        y_off = _dot(c_g, s_in.astype(BF16)) * ecs_x[:, c0:c0 + SSD_GROUP_WIDTH]
        state_ref[g] = s_in * chunk_decay_x[:, c0:c0 + SSD_GROUP_WIDTH] + new_state
        y_ref[:, c0:c0 + SSD_GROUP_WIDTH] = y_off + dskip_ref[:, c0:c0 + SSD_GROUP_WIDTH] * xs_g

    masked = []
    for hd in range(SSD_GROUPS * SSD_HEADS_PER_GROUP):
        seg = a_cs[:, hd:hd + 1] - a_cs_t[hd:hd + 1, :]
        decay = jnp.where(causal, jnp.exp(jnp.where(causal, seg, 0.0)), 0.0)
        masked.append((cbs[hd // SSD_HEADS_PER_GROUP] * decay).astype(BF16))

    lane = lax.broadcasted_iota(jnp.int32, (1, SSD_GROUP_WIDTH), 1)
    head_ind = [jnp.where(lane // SSD_HEAD_DIM == r, 1.0, 0.0).astype(BF16) for r in range(SSD_HEADS_PER_GROUP)]
    for g in range(SSD_GROUPS):
        c0 = g * SSD_GROUP_WIDTH
        y_diag = None
        for r in range(SSD_HEADS_PER_GROUP):
            part = _dot(masked[g * SSD_HEADS_PER_GROUP + r], xds[g] * head_ind[r])
            y_diag = part if y_diag is None else y_diag + part
        y_ref[:, c0:c0 + SSD_GROUP_WIDTH] += y_diag

    yz = y_ref[...] * _silu(z_ref[...].astype(F32))
    ms = jnp.mean(yz * yz, axis=-1, keepdims=True)
    o_ref[...] = (yz * lax.rsqrt(ms + EPS) * gain_ref[...]).astype(o_ref.dtype)


def _ssd(proj, dt_raw, conv_w, conv_b, dt_bias, a_log, d_skip_x, gain, expand, bsz, seq):
    q = SSD_CHUNK
    nc = seq // q
    d_inner = SSD_GROUPS * SSD_GROUP_WIDTH
    conv_dim = conv_w.shape[1]
    const = lambda b, c: (0, 0)
    hist = SSD_HISTORY_ROWS
    out_row = jnp.arange((SSD_CONV - 1) * q)[:, None]
    src_col = jnp.arange(hist + q)[None, :]
    shift = (src_col == hist + out_row % q - (out_row // q + 1)).astype(BF16)
    return pl.pallas_call(
        _ssd_kernel,
        grid=(bsz, nc),
        in_specs=[
            pl.BlockSpec((q, conv_dim), lambda b, c: (b * nc + c, 0)),
            pl.BlockSpec((q, d_inner), lambda b, c: (b * nc + c, conv_dim // d_inner)),
            pl.BlockSpec((q, LANES), lambda b, c: (b * nc + c, 0)),
            pl.BlockSpec((SSD_CONV, conv_dim), const),
            pl.BlockSpec((1, conv_dim), const),
            pl.BlockSpec((1, LANES), const),
            pl.BlockSpec((1, LANES), const),
            pl.BlockSpec((1, d_inner), const),
            pl.BlockSpec((1, d_inner), const),
            pl.BlockSpec((LANES, d_inner), const),
            pl.BlockSpec(((SSD_CONV - 1) * q, hist + q), const),
        ],
        out_specs=pl.BlockSpec((q, d_inner), lambda b, c: (b * nc + c, 0)),
        out_shape=jax.ShapeDtypeStruct((bsz * seq, d_inner), BF16),
        scratch_shapes=[
            pltpu.VMEM((hist + q, conv_dim), BF16),
            pltpu.VMEM((q, conv_dim), F32),
            pltpu.VMEM((SSD_GROUPS, SSD_STATE, SSD_GROUP_WIDTH), F32),
            pltpu.VMEM((q, d_inner), F32),
        ],
        compiler_params=_cparams("parallel", "arbitrary"),
        name="ssd",
    )(proj, proj, dt_raw, conv_w, conv_b, dt_bias, a_log, d_skip_x, gain, expand, shift)


def _dil_prep_kernel(in_ref, pos_ref, qg_ref, kg_ref, invf_ref, bd_ref, o_ref, cos_ref, sin_ref, tmp_ref):
    kind = pl.program_id(1)
    g = pl.program_id(2)
    rows = in_ref.shape[0]

    @pl.when(jnp.logical_and(kind == 0, g == 0))
    def _():
        ang = pos_ref[...].astype(F32) * invf_ref[...]
        lane = lax.broadcasted_iota(jnp.int32, (1, LANES), 1)
        first_half = (lane % DIL_HEAD_DIM) < (DIL_HEAD_DIM // 2)
        cos_ref[...] = jnp.cos(ang)
        sin_ref[...] = jnp.sin(ang) * jnp.where(first_half, -1.0, 1.0)

    @pl.when(kind < 2)
    def _():
        gain = jnp.where(kind == 0, qg_ref[...], kg_ref[...])
        scale = jnp.where(kind == 0, 1.0 / math.sqrt(DIL_HEAD_DIM), 1.0)
        lane = lax.broadcasted_iota(jnp.int32, (1, LANES), 1)
        first_half = (lane % DIL_HEAD_DIM) < (DIL_HEAD_DIM // 2)
        for c in range(DIL_GROUP_WIDTH // LANES):
            xc = in_ref[:, c * LANES:(c + 1) * LANES].astype(F32)
            hi, lo = _split2(xc * xc)
            ss = _dot(hi, bd_ref[...]) + _dot(lo, bd_ref[...])
            xn = xc * lax.rsqrt(ss * (1.0 / DIL_HEAD_DIM) + EPS) * gain
            partner = jnp.where(first_half, pltpu.roll(xn, LANES - DIL_HEAD_DIM // 2, 1),
                                pltpu.roll(xn, DIL_HEAD_DIM // 2, 1))
            tmp_ref[c] = (xn * cos_ref[...] + partner * sin_ref[...]) * scale

    @pl.when(kind == 2)
    def _():
        for c in range(DIL_GROUP_WIDTH // LANES):
            tmp_ref[c] = in_ref[:, c * LANES:(c + 1) * LANES].astype(F32)

    @pl.when(g == 0)
    def _():
        for c in range(DIL_GROUP_WIDTH // LANES):
            o_ref[:, c * LANES:(c + 1) * LANES] = tmp_ref[c].astype(o_ref.dtype)

    for gi in (1, 2):
        d = DIL_DILATIONS[gi]

        @pl.when(g == gi)
        def _(d=d):
            for unit in range(rows // DIL_BLOCK):
                s, rho = unit // d, unit % d
                for c in range(DIL_GROUP_WIDTH // LANES):
                    src = tmp_ref[c, pl.ds(s * DIL_BLOCK * d + rho, DIL_BLOCK, stride=d), :]
                    o_ref[unit * DIL_BLOCK:(unit + 1) * DIL_BLOCK, c * LANES:(c + 1) * LANES] = src.astype(o_ref.dtype)


def _dil_prep(proj, pos2d, q_gain, k_gain, inv_freq, blockdiag, first_col_block):
    t = proj.shape[0]
    rows = DIL_SUPER
    w = DIL_GROUP_WIDTH
    const = lambda i, k, g: (0, 0)
    return pl.pallas_call(
        _dil_prep_kernel,
        grid=(t // rows, 3, 3),
        in_specs=[
            pl.BlockSpec((rows, w), lambda i, k, g: (i, first_col_block + k * 3 + g)),
            pl.BlockSpec((rows, 1), lambda i, k, g: (i, 0)),
            pl.BlockSpec((1, LANES), const),
            pl.BlockSpec((1, LANES), const),
            pl.BlockSpec((1, LANES), const),
            pl.BlockSpec((LANES, LANES), const),
        ],
        out_specs=pl.BlockSpec((rows, w), lambda i, k, g: (i, k * 3 + g)),
        out_shape=jax.ShapeDtypeStruct((t, 9 * w), BF16),
        scratch_shapes=[pltpu.VMEM((rows, LANES), F32), pltpu.VMEM((rows, LANES), F32),
                        pltpu.VMEM((w // LANES, rows, LANES), F32)],
        compiler_params=_cparams("parallel", "arbitrary", "arbitrary"),
        name="dil_prep",
    )(proj, pos2d, q_gain, k_gain, inv_freq, blockdiag)


def _dil_attn_kernel(q_ref, kc_ref, kp_ref, vc_ref, vp_ref, o_ref, kbuf, vbuf, acc_ref, m_ref, l_ref):
    n = pl.program_id(1)
    g = pl.program_id(2)
    rows = q_ref.shape[0]
    blk = DIL_BLOCK
    n_units = rows // blk

    kbuf[0:rows, :] = kp_ref[...]
    kbuf[rows:2 * rows, :] = kc_ref[...]
    vbuf[0:rows, :] = vp_ref[...]
    vbuf[rows:2 * rows, :] = vc_ref[...]

    qi = lax.broadcasted_iota(jnp.int32, (blk, blk), 0)
    kj = lax.broadcasted_iota(jnp.int32, (blk, blk), 1)
    mask_cur = kj <= qi
    mask_prev = kj >= qi
    first_head = lax.broadcasted_iota(jnp.int32, (blk, LANES), 1) < DIL_HEAD_DIM
    head_ind = (jnp.where(first_head, 1.0, 0.0).astype(BF16), jnp.where(first_head, 0.0, 1.0).astype(BF16))

    def run_group(gi):
        d = DIL_DILATIONS[gi]

        def unit_body(u, carry):
            q_row = pl.multiple_of(u * blk, blk)
            c_row = pl.multiple_of(rows + u * blk, blk)
            p_row = pl.multiple_of(rows + u * blk - d * blk, blk)
            prev_bias = jnp.where(jnp.logical_or(n > 0, u >= d), 0.0, NEG)
            pos_row = (u // d) * (blk * d) + (u % d)
            pos_rows = pl.ds(pos_row, blk, stride=d) if d > 1 else pl.ds(pl.multiple_of(u * blk, blk), blk)
            n_hp = DIL_GROUP_WIDTH // LANES
            old = [] if gi == 0 else [(m_ref[hp, pos_rows, :], acc_ref[hp, pos_rows, :], l_ref[hp, pos_rows, :])
                                      for hp in range(n_hp)]
            n_heads = DIL_GROUP_WIDTH // DIL_HEAD_DIM
            head_lanes = [slice(h * DIL_HEAD_DIM, (h + 1) * DIL_HEAD_DIM) for h in range(n_heads)]
            scores = []
            for lanes in head_lanes:
                qh = q_ref[pl.ds(q_row, blk), lanes]
                s_cur = jnp.where(mask_cur, _dot_nt(qh, kbuf[pl.ds(c_row, blk), lanes]), NEG)
                s_prev = jnp.where(mask_prev, _dot_nt(qh, kbuf[pl.ds(p_row, blk), lanes]), NEG) + prev_bias
                scores.append((s_cur, s_prev))
            probs = []
            for s_cur, s_prev in scores:
                m_u = jnp.max(jnp.maximum(s_cur, s_prev), axis=1, keepdims=True)
                probs.append((jnp.exp(s_cur - m_u).astype(BF16), jnp.exp(s_prev - m_u).astype(BF16), m_u))
            new = []
            for hp in range(n_hp):
                pair = slice(hp * LANES, (hp + 1) * LANES)
                v_cur = vbuf[pl.ds(c_row, blk), pair]
                v_prev = vbuf[pl.ds(p_row, blk), pair]
                ol = None
                for hh, ind in enumerate(head_ind):
                    p_cur, p_prev, _ = probs[hp * 2 + hh]
                    rhs_cur = jnp.concatenate([v_cur * ind, ind], axis=1)
                    rhs_prev = jnp.concatenate([v_prev * ind, ind], axis=1)
                    part = _dot(p_cur, rhs_cur) + _dot(p_prev, rhs_prev)
                    ol = part if ol is None else ol + part
                m_pair = jnp.where(first_head, probs[hp * 2][2], probs[hp * 2 + 1][2])
                new.append((m_pair, ol[:, :LANES], ol[:, LANES:]))
            for hp in range(n_hp):
                m_u, o_u, l_u = new[hp]
                if gi == 0:
                    acc_ref[hp, pos_rows, :] = o_u
                    m_ref[hp, pos_rows, :] = m_u
                    l_ref[hp, pos_rows, :] = l_u
                else:
                    m_old, acc_old, l_old = old[hp]
                    m_new = jnp.maximum(m_old, m_u)
                    a_old = jnp.exp(m_old - m_new)
                    a_new = jnp.exp(m_u - m_new)
                    acc_ref[hp, pos_rows, :] = acc_old * a_old + o_u * a_new
                    l_ref[hp, pos_rows, :] = l_old * a_old + l_u * a_new
                    m_ref[hp, pos_rows, :] = m_new
            return carry

        lax.fori_loop(0, n_units, unit_body, 0)

    for gi in range(len(DIL_DILATIONS)):
        pl.when(g == gi)(functools.partial(run_group, gi))

    @pl.when(g == len(DIL_DILATIONS) - 1)
    def _():
        for hp in range(DIL_GROUP_WIDTH // LANES):
            o_ref[:, hp * LANES:(hp + 1) * LANES] = (acc_ref[hp] / l_ref[hp]).astype(o_ref.dtype)


def _dil_attn(qkv, bsz, seq):
    rows = DIL_SUPER
    w = DIL_GROUP_WIDTH
    nsb = seq // rows
    cur = lambda off: (lambda b, n, g: (b * nsb + n, off + g))
    prev = lambda off: (lambda b, n, g: (b * nsb + jnp.maximum(n - 1, 0), off + g))
    return pl.pallas_call(
        _dil_attn_kernel,
        grid=(bsz, nsb, 3),
        in_specs=[
            pl.BlockSpec((rows, w), cur(0)),
            pl.BlockSpec((rows, w), cur(3)),
            pl.BlockSpec((rows, w), prev(3)),
            pl.BlockSpec((rows, w), cur(6)),
            pl.BlockSpec((rows, w), prev(6)),
        ],
        out_specs=pl.BlockSpec((rows, w), lambda b, n, g: (b * nsb + n, 0)),
        out_shape=jax.ShapeDtypeStruct((bsz * seq, w), BF16),
        scratch_shapes=[pltpu.VMEM((2 * rows, w), BF16), pltpu.VMEM((2 * rows, w), BF16),
                        pltpu.VMEM((w // LANES, rows, LANES), F32), pltpu.VMEM((w // LANES, rows, LANES), F32),
                        pltpu.VMEM((w // LANES, rows, LANES), F32)],
        compiler_params=_cparams("parallel", "arbitrary", "arbitrary"),
        name="dil_attn",
    )(qkv, qkv, qkv, qkv, qkv)


def _mem_kv_kernel(mem_ref, g_ref, w_ref, kg_ref, k_ref, v_ref):
    x = mem_ref[...]
    ms = jnp.mean(x * x, axis=-1, keepdims=True)
    h = (x * lax.rsqrt(ms + EPS) * g_ref[...]).astype(BF16)
    kv = _dot(h, w_ref[...])
    width = MEM_HEADS * MEM_HEAD_DIM
    for hd in range(MEM_HEADS):
        sl = slice(hd * MEM_HEAD_DIM, (hd + 1) * MEM_HEAD_DIM)
        kh = kv[:, sl]
        ms = jnp.mean(kh * kh, axis=-1, keepdims=True)
        k_ref[:, sl] = (kh * lax.rsqrt(ms + EPS) * kg_ref[...]).astype(k_ref.dtype)
    v_ref[...] = kv[:, width:].astype(v_ref.dtype)


def _mem_kv(mem2d, gain, w_kv, k_gain, bsz, mem_len):
    d = mem2d.shape[1]
    width = MEM_HEADS * MEM_HEAD_DIM
    const = lambda b: (0, 0)
    return pl.pallas_call(
        _mem_kv_kernel,
        grid=(bsz,),
        in_specs=[
            pl.BlockSpec((mem_len, d), lambda b: (b, 0)),
            pl.BlockSpec((1, d), const),
            pl.BlockSpec((d, 2 * width), const),
            pl.BlockSpec((1, MEM_HEAD_DIM), const),
        ],
        out_specs=[pl.BlockSpec((mem_len, width), lambda b: (b, 0)),
                   pl.BlockSpec((mem_len, width), lambda b: (b, 0))],
        out_shape=[jax.ShapeDtypeStruct((bsz * mem_len, width), BF16)] * 2,
        compiler_params=_cparams("parallel"),
        name="mem_kv",
    )(mem2d, gain, w_kv, k_gain)


def _mem_attn_kernel(q_ref, k_ref, v_ref, qg_ref, o_ref):
    for hd in range(MEM_HEADS):
        sl = slice(hd * MEM_HEAD_DIM, (hd + 1) * MEM_HEAD_DIM)
        qh = q_ref[:, sl].astype(F32)
        ms = jnp.mean(qh * qh, axis=-1, keepdims=True)
        qn = (qh * lax.rsqrt(ms + EPS) * qg_ref[...]).astype(BF16)
        s = _dot_nt(qn, k_ref[:, sl]) * (1.0 / math.sqrt(MEM_HEAD_DIM))
        p = jnp.exp(s - jnp.max(s, axis=-1, keepdims=True))
        den = jnp.sum(p, axis=-1, keepdims=True)
        o_ref[:, sl] = (_dot(p.astype(BF16), v_ref[:, sl]) / den).astype(o_ref.dtype)


def _mem_attn(proj, k, v, q_gain, bsz, seq, mem_len, q_col_block, tq=512):
    width = MEM_HEADS * MEM_HEAD_DIM
    nq = seq // tq
    return pl.pallas_call(
        _mem_attn_kernel,
        grid=(bsz, nq),
        in_specs=[
            pl.BlockSpec((tq, width), lambda b, i: (b * nq + i, q_col_block)),
            pl.BlockSpec((mem_len, width), lambda b, i: (b, 0)),
            pl.BlockSpec((mem_len, width), lambda b, i: (b, 0)),
            pl.BlockSpec((1, MEM_HEAD_DIM), lambda b, i: (0, 0)),
        ],
        out_specs=pl.BlockSpec((tq, width), lambda b, i: (b * nq + i, 0)),
        out_shape=jax.ShapeDtypeStruct((bsz * seq, width), BF16),
        compiler_params=_cparams("parallel", "parallel"),
        name="mem_attn",
    )(proj, k, v, q_gain)


def _combine_kernel(ys_ref, yd_ref, ym_ref, gl_ref, x_ref, ws_ref, wd_ref, wm_ref, wo_ref, o_ref):
    d = x_ref.shape[1]
    gates = _sigmoid(gl_ref[...].astype(F32))
    merged = gates[:, 0:d] * _dot(ys_ref[...], ws_ref[...])
    merged = merged + gates[:, d:2 * d] * _dot(yd_ref[...], wd_ref[...])
    merged = merged + gates[:, 2 * d:3 * d] * _dot(ym_ref[...], wm_ref[...])
    o_ref[...] = x_ref[...] + _dot(merged.astype(BF16), wo_ref[...])


def _combine(y_ssd, y_dil, y_mem, proj, x2d, w_ssd, w_dil, w_mem, w_out, gate_col_block, tm=512):
    t, d = x2d.shape
    const = lambda i: (0, 0)
    row = lambda i: (i, 0)
    return pl.pallas_call(
        _combine_kernel,
        grid=(t // tm,),
        in_specs=[
            pl.BlockSpec((tm, y_ssd.shape[1]), row),
            pl.BlockSpec((tm, y_dil.shape[1]), row),
            pl.BlockSpec((tm, y_mem.shape[1]), row),
            pl.BlockSpec((tm, 3 * d), lambda i: (i, gate_col_block)),
            pl.BlockSpec((tm, d), row),
            pl.BlockSpec(w_ssd.shape, const),
            pl.BlockSpec(w_dil.shape, const),
            pl.BlockSpec(w_mem.shape, const),
            pl.BlockSpec(w_out.shape, const),
        ],
        out_specs=pl.BlockSpec((tm, d), row),
        out_shape=jax.ShapeDtypeStruct((t, d), F32),
        compiler_params=_cparams("parallel"),
        name="combine",
    )(y_ssd, y_dil, y_mem, proj, x2d, w_ssd, w_dil, w_mem, w_out)


def _top_rows(work, out_ref, count):
    for k in range(count):
        m = jnp.max(work, axis=0, keepdims=True)
        out_ref[k:k + 1, :] = m
        work = jnp.where(work == m, -jnp.inf, work)


def _gelu(x):
    return 0.5 * x * (1.0 + lax.erf(x * (1.0 / math.sqrt(2.0))))


def _peer_kernel(x_ref, gain_ref, wq_ref, k1_ref, k2_ref, u_ref, vt_ref, o_ref,
                 hn_ref, s1_ref, s2_ref, p1_ref, p2_ref, tau_ref, top1_ref, top2_ref, cand_ref, sc_ref,
                 g_ref, acc_ref):
    e = pl.program_id(1)
    n_e = pl.num_programs(1)
    tt = x_ref.shape[0]
    eb = u_ref.shape[0]
    k = PEER_TOPK
    nk = PEER_KEYS
    n_tc = tt // LANES
    n_i = eb // nk
    assert n_i == 8, "one aligned 8-row group of key-1 rows per expert block"

    @pl.when(e == 0)
    def _():
        x = x_ref[...]
        ms = jnp.mean(x * x, axis=-1, keepdims=True)
        hn = x * lax.rsqrt(ms + EPS) * gain_ref[...]
        hn_ref[...] = hn.T.astype(BF16)
        acc_ref[...] = jnp.zeros(acc_ref.shape, F32)
        for h in range(PEER_HEADS):
            r0 = h * 2 * PEER_HALF
            q1 = _dot(wq_ref[r0:r0 + PEER_HALF, :], hn_ref[...])
            q2 = _dot(wq_ref[r0 + PEER_HALF:r0 + 2 * PEER_HALF, :], hn_ref[...])
            s1 = _dot_f32(k1_ref[...], q1)
            s2 = _dot_f32(k2_ref[...], q2)
            _top_rows(s1, top1_ref, k)
            _top_rows(s2, top2_ref, k)
            v1 = top1_ref[...]
            v2 = top2_ref[...]
            brow = lax.broadcasted_iota(jnp.int32, (k, 1), 0)
            groups = []
            for a in range(k // 2):
                nb = k // (a + 1)
                rows_b = 8 if nb <= 8 else k
                blk = v1[a:a + 1, :] + v2[0:rows_b, :]
                groups.append(jnp.where(brow[0:rows_b] < nb, blk, -jnp.inf))
            groups.append(v1[k // 2:k, :] + v2[0:1, :])
            cand = jnp.concatenate(groups, axis=0)
            cand_ref[0:cand.shape[0], :] = cand
            _top_rows(cand_ref[0:cand.shape[0], :], sc_ref, k)
            sc = sc_ref[...]
            tau = jnp.broadcast_to(sc[k - 1:k, :], (8, tt))
            z = jnp.sum(jnp.exp(sc - sc[0:1, :]), axis=0, keepdims=True)
            p1 = jnp.exp(s1 - v1[0:1, :])
            p2 = jnp.exp(s2 - v2[0:1, :]) / z
            for tc in range(n_tc):
                ts = slice(tc * LANES, (tc + 1) * LANES)
                s1_ref[h, tc] = s1[:, ts]
                s2_ref[h, tc] = s2[:, ts]
                p1_ref[h, tc] = p1[:, ts]
                p2_ref[h, tc] = p2[:, ts]
                tau_ref[h, tc] = tau[:, ts]

    i0 = pl.multiple_of(e * n_i, n_i)

    for tc in range(n_tc):
        s1_rows = [s1_ref[h, tc, pl.ds(i0, n_i), :] for h in range(PEER_HEADS)]
        p1_rows = [p1_ref[h, tc, pl.ds(i0, n_i), :] for h in range(PEER_HEADS)]
        for ii in range(n_i):
            gsum = jnp.zeros((nk, LANES), F32)
            for h in range(PEER_HEADS):
                s = s1_rows[h][ii:ii + 1, :] + s2_ref[h, tc]
                pp = p1_rows[h][ii:ii + 1, :] * p2_ref[h, tc]
                gsum = gsum + jnp.where(s >= tau_ref[h, tc, 0:1, :], pp, 0.0)
            g_ref[tc, ii * nk:(ii + 1) * nk, :] = gsum

    act = _gelu(_dot(u_ref[...], hn_ref[...]))
    w = jnp.concatenate([(g_ref[tc] * act[:, tc * LANES:(tc + 1) * LANES]).astype(BF16) for tc in range(n_tc)],
                        axis=1)
    acc_ref[...] += _dot(vt_ref[...], w)

    @pl.when(e == n_e - 1)
    def _():
        o_ref[...] = x_ref[...] + acc_ref[...].T


def _peer(x2d, gain, wq_t, keys1, keys2, u_b, vt_b, tt=512, eb=1024):
    t, d = x2d.shape
    n_exp = u_b.shape[0]
    const = lambda i, e: (0, 0)
    n_cand = 8 * 10
    n_tc = tt // LANES
    return pl.pallas_call(
        _peer_kernel,
        grid=(t // tt, n_exp // eb),
        in_specs=[
            pl.BlockSpec((tt, d), lambda i, e: (i, 0)),
            pl.BlockSpec((1, d), const),
            pl.BlockSpec(wq_t.shape, const),
            pl.BlockSpec(keys1.shape, const),
            pl.BlockSpec(keys2.shape, const),
            pl.BlockSpec((eb, d), lambda i, e: (e, 0)),
            pl.BlockSpec((d, eb), lambda i, e: (0, e)),
        ],
        out_specs=pl.BlockSpec((tt, d), lambda i, e: (i, 0)),
        out_shape=jax.ShapeDtypeStruct((t, d), F32),
        scratch_shapes=[
            pltpu.VMEM((d, tt), BF16),
            pltpu.VMEM((PEER_HEADS, n_tc, PEER_KEYS, LANES), F32),
            pltpu.VMEM((PEER_HEADS, n_tc, PEER_KEYS, LANES), F32),
            pltpu.VMEM((PEER_HEADS, n_tc, PEER_KEYS, LANES), F32),
            pltpu.VMEM((PEER_HEADS, n_tc, PEER_KEYS, LANES), F32),
            pltpu.VMEM((PEER_HEADS, n_tc, 8, LANES), F32),
            pltpu.VMEM((PEER_TOPK, tt), F32),
            pltpu.VMEM((PEER_TOPK, tt), F32),
            pltpu.VMEM((n_cand, tt), F32),
            pltpu.VMEM((PEER_TOPK, tt), F32),
            pltpu.VMEM((n_tc, eb, LANES), F32),
            pltpu.VMEM((d, tt), F32),
        ],
        compiler_params=_cparams("parallel", "arbitrary"),
        name="peer",
    )(x2d, gain, wq_t, keys1, keys2, u_b, vt_b)


def _pad_lanes(v, fill=0.0):
    return jnp.pad(v, ((0, 0), (0, LANES - v.shape[1])), constant_values=fill)


def _layer(x2d, mem2d, pos2d, bsz, seq, mem_len, norm_mix, w_in, conv_w, conv_b, dt_bias, a_log, d_skip,
           ssd_norm, dil_q_norm, dil_k_norm, mem_norm, w_mem_kv, mem_q_norm, mem_k_norm,
           w_up_ssd, w_up_dil, w_up_mem, w_out, norm_ffn, peer_w_q, peer_keys1, peer_keys2, peer_u, peer_v):
    d_inner = SSD_GROUPS * SSD_GROUP_WIDTH
    conv_dim = d_inner + 2 * SSD_GROUPS * SSD_STATE
    n_heads = SSD_GROUPS * SSD_HEADS_PER_GROUP
    dil_w = 3 * DIL_GROUP_WIDTH
    mem_w = MEM_HEADS * MEM_HEAD_DIM
    d = x2d.shape[1]

    o = 0
    parts = {}
    for name, width in (("z", d_inner), ("xbc", conv_dim), ("dt", n_heads), ("q_d", dil_w), ("k_d", dil_w),
                        ("v_d", dil_w), ("q_m", mem_w), ("gates", 3 * d)):
        parts[name] = w_in[:, o:o + width]
        o += width
    w_main = jnp.concatenate([parts[k] for k in ("xbc", "z", "q_d", "k_d", "v_d", "q_m", "gates")],
                             axis=1).astype(BF16)
    w_dt = _pad_lanes(parts["dt"])
    z_off = conv_dim
    qd_off = z_off + d_inner
    qm_off = qd_off + 3 * dil_w
    gate_off = qm_off + mem_w

    proj, dt_raw = _in_proj(x2d, norm_mix[None, :], w_main, w_dt)

    heads = jnp.arange(LANES)[:, None]
    chans = jnp.arange(d_inner)[None, :]
    expand = (chans // SSD_HEAD_DIM == heads).astype(BF16)
    y_ssd = _ssd(proj, dt_raw, conv_w, conv_b[None, :], _pad_lanes(dt_bias[None, :]), _pad_lanes(a_log[None, :]),
                 jnp.repeat(d_skip, SSD_HEAD_DIM)[None, :], ssd_norm[None, :], expand, bsz, seq)

    half = DIL_HEAD_DIM // 2
    inv_freq = ROPE_THETA ** (-jnp.arange(half, dtype=F32) / half)
    inv_freq = jnp.tile(inv_freq, LANES // half)[None, :]
    lane = jnp.arange(LANES)
    blockdiag = (lane[:, None] // DIL_HEAD_DIM == lane[None, :] // DIL_HEAD_DIM).astype(BF16)
    qkv = _dil_prep(proj, pos2d, jnp.tile(dil_q_norm, LANES // DIL_HEAD_DIM)[None, :],
                    jnp.tile(dil_k_norm, LANES // DIL_HEAD_DIM)[None, :], inv_freq, blockdiag,
                    qd_off // DIL_GROUP_WIDTH)
    y_dil = _dil_attn(qkv, bsz, seq)

    k_mem, v_mem = _mem_kv(mem2d, mem_norm[None, :], w_mem_kv.astype(BF16), mem_k_norm[None, :], bsz, mem_len)
    y_mem = _mem_attn(proj, k_mem, v_mem, mem_q_norm[None, :], bsz, seq, mem_len, qm_off // mem_w)

    x_mid = _combine(y_ssd, y_dil, y_mem, proj, x2d, w_up_ssd.astype(BF16), w_up_dil.astype(BF16),
                     w_up_mem.astype(BF16), w_out.astype(BF16), gate_off // (3 * d))

    return _peer(x_mid, norm_ffn[None, :], peer_w_q.T.astype(BF16), peer_keys1, peer_keys2,
                 peer_u.astype(BF16), peer_v.T.astype(BF16))


def kernel(x, mem, positions, norm_mix, w_in, conv_w, conv_b, dt_bias, a_log, d_skip, ssd_norm, dil_q_norm,
           dil_k_norm, mem_norm, w_mem_kv, mem_q_norm, mem_k_norm, w_up_ssd, w_up_dil, w_up_mem, w_out,
           norm_ffn, peer_w_q, peer_keys1, peer_keys2, peer_u, peer_v):
    bsz, seq, d = x.shape
    mem_len = mem.shape[1]
    x2d = x.reshape(bsz * seq, d)
    mem2d = mem.reshape(bsz * mem_len, d)
    pos2d = positions.reshape(bsz * seq, 1)
    for layer in range(norm_mix.shape[0]):
        x2d = _layer(x2d, mem2d, pos2d, bsz, seq, mem_len, norm_mix[layer], w_in[layer], conv_w[layer],
                     conv_b[layer], dt_bias[layer], a_log[layer], d_skip[layer], ssd_norm[layer],
                     dil_q_norm[layer], dil_k_norm[layer], mem_norm[layer], w_mem_kv[layer], mem_q_norm[layer],
                     mem_k_norm[layer], w_up_ssd[layer], w_up_dil[layer], w_up_mem[layer], w_out[layer],
                     norm_ffn[layer], peer_w_q[layer], peer_keys1[layer], peer_keys2[layer], peer_u[layer],
                     peer_v[layer])
    return x2d.reshape(bsz, seq, d)
```

```python
import functools
import math

import jax
import jax.numpy as jnp
from jax import lax
from jax.experimental import pallas as pl
from jax.experimental.pallas import tpu as pltpu

F32 = jnp.float32
BF16 = jnp.bfloat16
EPS = 1e-6
NEG = -1e30

VMEM_LIMIT_BYTES = 56 * 1024 * 1024
LANES = 128

SSD_CHUNK = 128
SSD_HEAD_DIM = 64
SSD_STATE = 128
SSD_GROUPS = 8
SSD_HEADS_PER_GROUP = 4
SSD_GROUP_WIDTH = SSD_HEADS_PER_GROUP * SSD_HEAD_DIM
SSD_CONV = 4
SSD_HISTORY_ROWS = 16
SSD_CONV_COLS = 512

DIL_DILATIONS = (1, 4, 16)
DIL_BLOCK = 128
DIL_SUPER = DIL_BLOCK * 16
DIL_HEAD_DIM = 64
DIL_GROUP_WIDTH = 8 * DIL_HEAD_DIM
ROPE_THETA = 10000.0

MEM_HEADS = 4
MEM_HEAD_DIM = 384

PEER_KEYS = 128
PEER_HEADS = 8
PEER_TOPK = 16
PEER_HALF = 128
PEER_UNRANKED = 255.0
BF16_SUBLANES = 16
PEER_ROWS_PER_PASS = 1
PEER_GROUP_TOKENS = 256


def _cparams(*sem):
    return pltpu.CompilerParams(dimension_semantics=sem, vmem_limit_bytes=VMEM_LIMIT_BYTES)


def _dot(a, b):
    return jnp.dot(a, b, preferred_element_type=F32)


def _dot_nt(a, b):
    return lax.dot_general(a, b, (((1,), (1,)), ((), ())), preferred_element_type=F32)


def _split2(x):
    hi = x.astype(BF16)
    lo = (x - hi.astype(F32)).astype(BF16)
    return hi, lo


def _split3(x):
    hi = x.astype(BF16)
    r = x - hi.astype(F32)
    mid = r.astype(BF16)
    lo = (r - mid.astype(F32)).astype(BF16)
    return hi, mid, lo


def _dot_f32(a, b):
    ah, al = _split2(a)
    bh, bl = _split2(b)
    return _dot(ah, bh) + (_dot(ah, bl) + _dot(al, bh))


def _dot_sel_right(m, e):
    hi, mid, lo = _split3(m)
    return _dot(hi, e) + (_dot(mid, e) + _dot(lo, e))


def _dot_sel_left(e, m):
    hi, mid, lo = _split3(m)
    return _dot(e, hi) + (_dot(e, mid) + _dot(e, lo))


def _sigmoid(x):
    return 1.0 / (1.0 + jnp.exp(-x))


def _silu(x):
    return x * _sigmoid(x)


def _in_proj_kernel(x_ref, g_ref, w_ref, wdt_ref, o_ref, dt_ref, h_ref):
    @pl.when(pl.program_id(1) == 0)
    def _():
        x = x_ref[...]
        ms = jnp.mean(x * x, axis=-1, keepdims=True)
        h = x * lax.rsqrt(ms + EPS) * g_ref[...]
        h_ref[...] = h.astype(BF16)
        dt_ref[...] = _dot_f32(h, wdt_ref[...])

    o_ref[...] = _dot(h_ref[...], w_ref[...]).astype(o_ref.dtype)


def _in_proj(x2d, gain, w_main, w_dt, tm=2048, tn=512):
    t, d = x2d.shape
    n = w_main.shape[1]
    return pl.pallas_call(
        _in_proj_kernel,
        grid=(t // tm, n // tn),
        in_specs=[
            pl.BlockSpec((tm, d), lambda i, j: (i, 0)),
            pl.BlockSpec((1, d), lambda i, j: (0, 0)),
            pl.BlockSpec((d, tn), lambda i, j: (0, j)),
            pl.BlockSpec((d, LANES), lambda i, j: (0, 0)),
        ],
        out_specs=[
            pl.BlockSpec((tm, tn), lambda i, j: (i, j)),
            pl.BlockSpec((tm, LANES), lambda i, j: (i, 0)),
        ],
        out_shape=[jax.ShapeDtypeStruct((t, n), BF16), jax.ShapeDtypeStruct((t, LANES), F32)],
        scratch_shapes=[pltpu.VMEM((tm, d), BF16)],
        compiler_params=_cparams("parallel", "arbitrary"),
        name="in_proj",
    )(x2d, gain, w_main, w_dt)


def _ssd_kernel(xbc_ref, z_ref, dt_ref, cw_ref, cb_ref, dtb_ref, alog_ref, dskip_ref, gain_ref, e_ref, shift_ref,
                o_ref, xh_ref, u_ref, state_ref, y_ref):
    q = SSD_CHUNK
    d_inner = SSD_GROUPS * SSD_GROUP_WIDTH
    gn = SSD_GROUPS * SSD_STATE

    hist = SSD_HISTORY_ROWS

    @pl.when(pl.program_id(1) == 0)
    def _():
        xh_ref[0:hist, :] = jnp.zeros((hist, xh_ref.shape[1]), BF16)
        state_ref[...] = jnp.zeros(state_ref.shape, F32)

    xh_ref[hist:hist + q, :] = xbc_ref[...]
    for cblk in range(xh_ref.shape[1] // SSD_CONV_COLS):
        cs = slice(cblk * SSD_CONV_COLS, (cblk + 1) * SSD_CONV_COLS)
        delayed = _dot(shift_ref[...], xh_ref[:, cs])
        acc = cb_ref[:, cs] + cw_ref[3:4, cs] * xbc_ref[:, cs].astype(F32)
        for k in range(1, SSD_CONV):
            acc = acc + cw_ref[SSD_CONV - 1 - k:SSD_CONV - k, cs] * delayed[(k - 1) * q:k * q, :]
        u_ref[:, cs] = _silu(acc)
    xh_ref[0:hist, :] = xh_ref[q:q + hist, :]

    pre = dt_ref[...] + dtb_ref[...]
    dt = jnp.maximum(pre, 0.0) + jnp.log1p(jnp.exp(-jnp.abs(pre)))
    adt = dt * (-jnp.exp(alog_ref[...]))
    row = lax.broadcasted_iota(jnp.int32, (q, q), 0)
    col = lax.broadcasted_iota(jnp.int32, (q, q), 1)
    causal = row >= col
    tril = jnp.where(causal, 1.0, 0.0).astype(BF16)
    a_cs = _dot_sel_left(tril, adt)
    a_cs_t = a_cs.T
    dt_x = _dot(dt.astype(BF16), e_ref[...])
    acs_x = _dot_sel_right(a_cs, e_ref[...])
    ecs_x = jnp.exp(acs_x)
    dstate_x = jnp.exp(acs_x[q - 1:q, :] - acs_x)
    chunk_decay_x = ecs_x[q - 1:q, :]

    cbs, xds = [], []
    for g in range(SSD_GROUPS):
        c0 = g * SSD_GROUP_WIDTH
        xs_g = u_ref[:, c0:c0 + SSD_GROUP_WIDTH]
        b_g = u_ref[:, d_inner + g * SSD_STATE:d_inner + (g + 1) * SSD_STATE]
        c_g = u_ref[:, d_inner + gn + g * SSD_STATE:d_inner + gn + (g + 1) * SSD_STATE].astype(BF16)
        b_gt = b_g.T.astype(BF16)
        cbs.append(_dot(c_g, b_gt))
        xd = xs_g * dt_x[:, c0:c0 + SSD_GROUP_WIDTH]
        xds.append(xd.astype(BF16))
        xs_b = (xd * dstate_x[:, c0:c0 + SSD_GROUP_WIDTH]).astype(BF16)
        new_state = _dot(b_gt, xs_b)
        s_in = state_ref[g]
        y_off = _dot(c_g, s_in.astype(BF16)) * ecs_x[:, c0:c0 + SSD_GROUP_WIDTH]
        state_ref[g] = s_in * chunk_decay_x[:, c0:c0 + SSD_GROUP_WIDTH] + new_state
        y_ref[:, c0:c0 + SSD_GROUP_WIDTH] = y_off + dskip_ref[:, c0:c0 + SSD_GROUP_WIDTH] * xs_g

    masked = []
    for hd in range(SSD_GROUPS * SSD_HEADS_PER_GROUP):
        seg = a_cs[:, hd:hd + 1] - a_cs_t[hd:hd + 1, :]
        decay = jnp.where(causal, jnp.exp(jnp.where(causal, seg, 0.0)), 0.0)
        masked.append((cbs[hd // SSD_HEADS_PER_GROUP] * decay).astype(BF16))

    lane = lax.broadcasted_iota(jnp.int32, (1, SSD_GROUP_WIDTH), 1)
    head_ind = [jnp.where(lane // SSD_HEAD_DIM == r, 1.0, 0.0).astype(BF16) for r in range(SSD_HEADS_PER_GROUP)]
    for g in range(SSD_GROUPS):
        c0 = g * SSD_GROUP_WIDTH
        y_diag = None
        for r in range(SSD_HEADS_PER_GROUP):
            part = _dot(masked[g * SSD_HEADS_PER_GROUP + r], xds[g] * head_ind[r])
            y_diag = part if y_diag is None else y_diag + part
        y_ref[:, c0:c0 + SSD_GROUP_WIDTH] += y_diag

    yz = y_ref[...] * _silu(z_ref[...].astype(F32))
    ms = jnp.mean(yz * yz, axis=-1, keepdims=True)
    o_ref[...] = (yz * lax.rsqrt(ms + EPS) * gain_ref[...]).astype(o_ref.dtype)


def _ssd(proj, dt_raw, conv_w, conv_b, dt_bias, a_log, d_skip_x, gain, expand, bsz, seq):
    q = SSD_CHUNK
    nc = seq // q
    d_inner = SSD_GROUPS * SSD_GROUP_WIDTH
    conv_dim = conv_w.shape[1]
    const = lambda b, c: (0, 0)
    hist = SSD_HISTORY_ROWS
    out_row = jnp.arange((SSD_CONV - 1) * q)[:, None]
    src_col = jnp.arange(hist + q)[None, :]
    shift = (src_col == hist + out_row % q - (out_row // q + 1)).astype(BF16)
    return pl.pallas_call(
        _ssd_kernel,
        grid=(bsz, nc),
        in_specs=[
            pl.BlockSpec((q, conv_dim), lambda b, c: (b * nc + c, 0)),
            pl.BlockSpec((q, d_inner), lambda b, c: (b * nc + c, conv_dim // d_inner)),
            pl.BlockSpec((q, LANES), lambda b, c: (b * nc + c, 0)),
            pl.BlockSpec((SSD_CONV, conv_dim), const),
            pl.BlockSpec((1, conv_dim), const),
            pl.BlockSpec((1, LANES), const),
            pl.BlockSpec((1, LANES), const),
            pl.BlockSpec((1, d_inner), const),
            pl.BlockSpec((1, d_inner), const),
            pl.BlockSpec((LANES, d_inner), const),
            pl.BlockSpec(((SSD_CONV - 1) * q, hist + q), const),
        ],
        out_specs=pl.BlockSpec((q, d_inner), lambda b, c: (b * nc + c, 0)),
        out_shape=jax.ShapeDtypeStruct((bsz * seq, d_inner), BF16),
        scratch_shapes=[
            pltpu.VMEM((hist + q, conv_dim), BF16),
            pltpu.VMEM((q, conv_dim), F32),
            pltpu.VMEM((SSD_GROUPS, SSD_STATE, SSD_GROUP_WIDTH), F32),
            pltpu.VMEM((q, d_inner), F32),
        ],
        compiler_params=_cparams("parallel", "arbitrary"),
        name="ssd",
    )(proj, proj, dt_raw, conv_w, conv_b, dt_bias, a_log, d_skip_x, gain, expand, shift)


def _dil_prep_kernel(in_ref, pos_ref, qg_ref, kg_ref, invf_ref, bd_ref, o_ref, cos_ref, sin_ref, tmp_ref):
    kind = pl.program_id(1)
    g = pl.program_id(2)
    rows = in_ref.shape[0]

    @pl.when(jnp.logical_and(kind == 0, g == 0))
    def _():
        ang = pos_ref[...].astype(F32) * invf_ref[...]
        lane = lax.broadcasted_iota(jnp.int32, (1, LANES), 1)
        first_half = (lane % DIL_HEAD_DIM) < (DIL_HEAD_DIM // 2)
        cos_ref[...] = jnp.cos(ang)
        sin_ref[...] = jnp.sin(ang) * jnp.where(first_half, -1.0, 1.0)

    @pl.when(kind < 2)
    def _():
        gain = jnp.where(kind == 0, qg_ref[...] * (1.0 / math.sqrt(DIL_HEAD_DIM)), kg_ref[...])
        lane = lax.broadcasted_iota(jnp.int32, (1, LANES), 1)
        first_half = (lane % DIL_HEAD_DIM) < (DIL_HEAD_DIM // 2)
        for c in range(DIL_GROUP_WIDTH // LANES):
            xc = in_ref[:, c * LANES:(c + 1) * LANES].astype(F32)
            ss = _dot((xc * xc).astype(BF16), bd_ref[...])
            xn = xc * lax.rsqrt(ss * (1.0 / DIL_HEAD_DIM) + EPS) * gain
            partner = jnp.where(first_half, pltpu.roll(xn, LANES - DIL_HEAD_DIM // 2, 1),
                                pltpu.roll(xn, DIL_HEAD_DIM // 2, 1))
            tmp_ref[c] = xn * cos_ref[...] + partner * sin_ref[...]

    @pl.when(kind == 2)
    def _():
        for c in range(DIL_GROUP_WIDTH // LANES):
            tmp_ref[c] = in_ref[:, c * LANES:(c + 1) * LANES].astype(F32)

    @pl.when(g == 0)
    def _():
        for c in range(DIL_GROUP_WIDTH // LANES):
            o_ref[:, c * LANES:(c + 1) * LANES] = tmp_ref[c].astype(o_ref.dtype)

    for gi in (1, 2):
        d = DIL_DILATIONS[gi]

        @pl.when(g == gi)
        def _(d=d):
            for unit in range(rows // DIL_BLOCK):
                s, rho = unit // d, unit % d
                for c in range(DIL_GROUP_WIDTH // LANES):
                    src = tmp_ref[c, pl.ds(s * DIL_BLOCK * d + rho, DIL_BLOCK, stride=d), :]
                    o_ref[unit * DIL_BLOCK:(unit + 1) * DIL_BLOCK, c * LANES:(c + 1) * LANES] = src.astype(o_ref.dtype)


def _dil_prep(proj, pos2d, q_gain, k_gain, inv_freq, blockdiag, first_col_block):
    t = proj.shape[0]
    rows = DIL_SUPER
    w = DIL_GROUP_WIDTH
    const = lambda i, k, g: (0, 0)
    return pl.pallas_call(
        _dil_prep_kernel,
        grid=(t // rows, 3, 3),
        in_specs=[
            pl.BlockSpec((rows, w), lambda i, k, g: (i, first_col_block + k * 3 + g)),
            pl.BlockSpec((rows, 1), lambda i, k, g: (i, 0)),
            pl.BlockSpec((1, LANES), const),
            pl.BlockSpec((1, LANES), const),
            pl.BlockSpec((1, LANES), const),
            pl.BlockSpec((LANES, LANES), const),
        ],
        out_specs=pl.BlockSpec((rows, w), lambda i, k, g: (i, k * 3 + g)),
        out_shape=jax.ShapeDtypeStruct((t, 9 * w), BF16),
        scratch_shapes=[pltpu.VMEM((rows, LANES), F32), pltpu.VMEM((rows, LANES), F32),
                        pltpu.VMEM((w // LANES, rows, LANES), F32)],
        compiler_params=_cparams("parallel", "arbitrary", "arbitrary"),
        name="dil_prep",
    )(proj, pos2d, q_gain, k_gain, inv_freq, blockdiag)


def _dil_attn_kernel(q_ref, kc_ref, kp_ref, vc_ref, vp_ref, o_ref, kbuf, vbuf, acc_ref, m_ref, l_ref):
    n = pl.program_id(1)
    g = pl.program_id(2)
    rows = q_ref.shape[0]
    blk = DIL_BLOCK
    n_units = rows // blk

    kbuf[0:rows, :] = kp_ref[...]
    kbuf[rows:2 * rows, :] = kc_ref[...]
    vbuf[0:rows, :] = vp_ref[...]
    vbuf[rows:2 * rows, :] = vc_ref[...]

    qi = lax.broadcasted_iota(jnp.int32, (blk, blk), 0)
    kj = lax.broadcasted_iota(jnp.int32, (blk, blk), 1)
    mask_cur = kj <= qi
    mask_prev = kj >= qi
    first_head = lax.broadcasted_iota(jnp.int32, (blk, LANES), 1) < DIL_HEAD_DIM
    head_ind = (jnp.where(first_head, 1.0, 0.0).astype(BF16), jnp.where(first_head, 0.0, 1.0).astype(BF16))

    def run_group(gi):
        d = DIL_DILATIONS[gi]

        def unit_body(u, carry):
            q_row = pl.multiple_of(u * blk, blk)
            c_row = pl.multiple_of(rows + u * blk, blk)
            p_row = pl.multiple_of(rows + u * blk - d * blk, blk)
            prev_bias = jnp.where(jnp.logical_or(n > 0, u >= d), 0.0, NEG)
            pos_row = (u // d) * (blk * d) + (u % d)
            pos_rows = pl.ds(pos_row, blk, stride=d) if d > 1 else pl.ds(pl.multiple_of(u * blk, blk), blk)
            n_hp = DIL_GROUP_WIDTH // LANES
            old = [] if gi == 0 else [(m_ref[hp, pos_rows, :], acc_ref[hp, pos_rows, :], l_ref[hp, pos_rows, :])
                                      for hp in range(n_hp)]
            n_heads = DIL_GROUP_WIDTH // DIL_HEAD_DIM
            head_lanes = [slice(h * DIL_HEAD_DIM, (h + 1) * DIL_HEAD_DIM) for h in range(n_heads)]
            scores = []
            for lanes in head_lanes:
                qh = q_ref[pl.ds(q_row, blk), lanes]
                s_cur = jnp.where(mask_cur, _dot_nt(qh, kbuf[pl.ds(c_row, blk), lanes]), NEG)
                s_prev = jnp.where(mask_prev, _dot_nt(qh, kbuf[pl.ds(p_row, blk), lanes]), NEG) + prev_bias
                scores.append((s_cur, s_prev))
            probs = []
            for s_cur, s_prev in scores:
                m_u = jnp.max(jnp.maximum(s_cur, s_prev), axis=1, keepdims=True)
                probs.append((jnp.exp(s_cur - m_u).astype(BF16), jnp.exp(s_prev - m_u).astype(BF16), m_u))
            new = []
            for hp in range(n_hp):
                pair = slice(hp * LANES, (hp + 1) * LANES)
                v_cur = vbuf[pl.ds(c_row, blk), pair]
                v_prev = vbuf[pl.ds(p_row, blk), pair]
                ol = None
                for hh, ind in enumerate(head_ind):
                    p_cur, p_prev, _ = probs[hp * 2 + hh]
                    rhs_cur = jnp.concatenate([v_cur * ind, ind], axis=1)
                    rhs_prev = jnp.concatenate([v_prev * ind, ind], axis=1)
                    part = _dot(p_cur, rhs_cur) + _dot(p_prev, rhs_prev)
                    ol = part if ol is None else ol + part
                m_pair = jnp.where(first_head, probs[hp * 2][2], probs[hp * 2 + 1][2])
                new.append((m_pair, ol[:, :LANES], ol[:, LANES:]))
            for hp in range(n_hp):
                m_u, o_u, l_u = new[hp]
                if gi == 0:
                    acc_ref[hp, pos_rows, :] = o_u
                    m_ref[hp, pos_rows, :] = m_u
                    l_ref[hp, pos_rows, :] = l_u
                else:
                    m_old, acc_old, l_old = old[hp]
                    m_new = jnp.maximum(m_old, m_u)
                    a_old = jnp.exp(m_old - m_new)
                    a_new = jnp.exp(m_u - m_new)
                    acc_ref[hp, pos_rows, :] = acc_old * a_old + o_u * a_new
                    l_ref[hp, pos_rows, :] = l_old * a_old + l_u * a_new
                    m_ref[hp, pos_rows, :] = m_new
            return carry

        lax.fori_loop(0, n_units, unit_body, 0)

    for gi in range(len(DIL_DILATIONS)):
        pl.when(g == gi)(functools.partial(run_group, gi))

    @pl.when(g == len(DIL_DILATIONS) - 1)
    def _():
        for hp in range(DIL_GROUP_WIDTH // LANES):
            o_ref[:, hp * LANES:(hp + 1) * LANES] = (acc_ref[hp] / l_ref[hp]).astype(o_ref.dtype)


def _dil_attn(qkv, bsz, seq):
    rows = DIL_SUPER
    w = DIL_GROUP_WIDTH
    nsb = seq // rows
    cur = lambda off: (lambda b, n, g: (b * nsb + n, off + g))
    prev = lambda off: (lambda b, n, g: (b * nsb + jnp.maximum(n - 1, 0), off + g))
    return pl.pallas_call(
        _dil_attn_kernel,
        grid=(bsz, nsb, 3),
        in_specs=[
            pl.BlockSpec((rows, w), cur(0)),
            pl.BlockSpec((rows, w), cur(3)),
            pl.BlockSpec((rows, w), prev(3)),
            pl.BlockSpec((rows, w), cur(6)),
            pl.BlockSpec((rows, w), prev(6)),
        ],
        out_specs=pl.BlockSpec((rows, w), lambda b, n, g: (b * nsb + n, 0)),
        out_shape=jax.ShapeDtypeStruct((bsz * seq, w), BF16),
        scratch_shapes=[pltpu.VMEM((2 * rows, w), BF16), pltpu.VMEM((2 * rows, w), BF16),
                        pltpu.VMEM((w // LANES, rows, LANES), F32), pltpu.VMEM((w // LANES, rows, LANES), F32),
                        pltpu.VMEM((w // LANES, rows, LANES), F32)],
        compiler_params=_cparams("parallel", "arbitrary", "arbitrary"),
        name="dil_attn",
    )(qkv, qkv, qkv, qkv, qkv)


def _mem_kv_kernel(mem_ref, g_ref, w_ref, kg_ref, k_ref, v_ref):
    x = mem_ref[...]
    ms = jnp.mean(x * x, axis=-1, keepdims=True)
    h = (x * lax.rsqrt(ms + EPS) * g_ref[...]).astype(BF16)
    kv = _dot(h, w_ref[...])
    width = MEM_HEADS * MEM_HEAD_DIM
    for hd in range(MEM_HEADS):
        sl = slice(hd * MEM_HEAD_DIM, (hd + 1) * MEM_HEAD_DIM)
        kh = kv[:, sl]
        ms = jnp.mean(kh * kh, axis=-1, keepdims=True)
        k_ref[:, sl] = (kh * lax.rsqrt(ms + EPS) * kg_ref[...]).astype(k_ref.dtype)
    v_ref[...] = kv[:, width:].astype(v_ref.dtype)


def _mem_kv(mem2d, gain, w_kv, k_gain, bsz, mem_len):
    d = mem2d.shape[1]
    width = MEM_HEADS * MEM_HEAD_DIM
    const = lambda b: (0, 0)
    return pl.pallas_call(
        _mem_kv_kernel,
        grid=(bsz,),
        in_specs=[
            pl.BlockSpec((mem_len, d), lambda b: (b, 0)),
            pl.BlockSpec((1, d), const),
            pl.BlockSpec((d, 2 * width), const),
            pl.BlockSpec((1, MEM_HEAD_DIM), const),
        ],
        out_specs=[pl.BlockSpec((mem_len, width), lambda b: (b, 0)),
                   pl.BlockSpec((mem_len, width), lambda b: (b, 0))],
        out_shape=[jax.ShapeDtypeStruct((bsz * mem_len, width), BF16)] * 2,
        compiler_params=_cparams("parallel"),
        name="mem_kv",
    )(mem2d, gain, w_kv, k_gain)


def _mem_attn_kernel(q_ref, k_ref, v_ref, qg_ref, o_ref):
    for hd in range(MEM_HEADS):
        sl = slice(hd * MEM_HEAD_DIM, (hd + 1) * MEM_HEAD_DIM)
        qh = q_ref[:, sl].astype(F32)
        ms = jnp.mean(qh * qh, axis=-1, keepdims=True)
        qn = (qh * lax.rsqrt(ms + EPS) * qg_ref[...]).astype(BF16)
        s = _dot_nt(qn, k_ref[:, sl]) * (1.0 / math.sqrt(MEM_HEAD_DIM))
        p = jnp.exp(s - jnp.max(s, axis=-1, keepdims=True))
        den = jnp.sum(p, axis=-1, keepdims=True)
        o_ref[:, sl] = (_dot(p.astype(BF16), v_ref[:, sl]) / den).astype(o_ref.dtype)


def _mem_attn(proj, k, v, q_gain, bsz, seq, mem_len, q_col_block, tq=512):
    width = MEM_HEADS * MEM_HEAD_DIM
    nq = seq // tq
    return pl.pallas_call(
        _mem_attn_kernel,
        grid=(bsz, nq),
        in_specs=[
            pl.BlockSpec((tq, width), lambda b, i: (b * nq + i, q_col_block)),
            pl.BlockSpec((mem_len, width), lambda b, i: (b, 0)),
            pl.BlockSpec((mem_len, width), lambda b, i: (b, 0)),
            pl.BlockSpec((1, MEM_HEAD_DIM), lambda b, i: (0, 0)),
        ],
        out_specs=pl.BlockSpec((tq, width), lambda b, i: (b * nq + i, 0)),
        out_shape=jax.ShapeDtypeStruct((bsz * seq, width), BF16),
        compiler_params=_cparams("parallel", "parallel"),
        name="mem_attn",
    )(proj, k, v, q_gain)


def _combine_kernel(ys_ref, yd_ref, ym_ref, gl_ref, x_ref, ws_ref, wd_ref, wm_ref, wo_ref, o_ref):
    d = x_ref.shape[1]
    gates = _sigmoid(gl_ref[...].astype(F32))
    merged = gates[:, 0:d] * _dot(ys_ref[...], ws_ref[...])
    merged = merged + gates[:, d:2 * d] * _dot(yd_ref[...], wd_ref[...])
    merged = merged + gates[:, 2 * d:3 * d] * _dot(ym_ref[...], wm_ref[...])
    o_ref[...] = x_ref[...] + _dot(merged.astype(BF16), wo_ref[...])


def _combine(y_ssd, y_dil, y_mem, proj, x2d, w_ssd, w_dil, w_mem, w_out, gate_col_block, tm=512):
    t, d = x2d.shape
    const = lambda i: (0, 0)
    row = lambda i: (i, 0)
    return pl.pallas_call(
        _combine_kernel,
        grid=(t // tm,),
        in_specs=[
            pl.BlockSpec((tm, y_ssd.shape[1]), row),
            pl.BlockSpec((tm, y_dil.shape[1]), row),
            pl.BlockSpec((tm, y_mem.shape[1]), row),
            pl.BlockSpec((tm, 3 * d), lambda i: (i, gate_col_block)),
            pl.BlockSpec((tm, d), row),
            pl.BlockSpec(w_ssd.shape, const),
            pl.BlockSpec(w_dil.shape, const),
            pl.BlockSpec(w_mem.shape, const),
            pl.BlockSpec(w_out.shape, const),
        ],
        out_specs=pl.BlockSpec((tm, d), row),
        out_shape=jax.ShapeDtypeStruct((t, d), F32),
        compiler_params=_cparams("parallel"),
        name="combine",
    )(y_ssd, y_dil, y_mem, proj, x2d, w_ssd, w_dil, w_mem, w_out)


RANK_CODE = -(2.0 ** 100)


def _top_rows(work, out_ref, count):
    for r in range(count):
        m = jnp.max(work, axis=0, keepdims=True)
        out_ref[r:r + 1, :] = m
        work = jnp.where(work == m, (r + 1) * RANK_CODE, work)
    return work


def _gelu(x):
    return 0.5 * x * (1.0 + lax.erf(x * (1.0 / math.sqrt(2.0))))


def _peer_kernel(x_ref, gain_ref, wq_ref, k1_ref, k2_ref, u_ref, vt_ref, o_ref,
                 hn_ref, cnt_ref, rank2_ref, p1_ref, p2_ref, top1_ref, top2_ref, cand_ref, sc_ref,
                 g_ref, acc_ref):
    e = pl.program_id(1)
    n_e = pl.num_programs(1)
    tt = x_ref.shape[0]
    eb = u_ref.shape[0]
    k = PEER_TOPK
    nk = PEER_KEYS
    n_tc = tt // LANES
    n_grp = tt // PEER_GROUP_TOKENS
    n_i = eb // nk
    assert n_i == 8, "one aligned 8-row group of key-1 rows per expert block"

    @pl.when(e == 0)
    def _():
        x = x_ref[...]
        ms = jnp.mean(x * x, axis=-1, keepdims=True)
        hn = x * lax.rsqrt(ms + EPS) * gain_ref[...]
        hn_t = hn.T.astype(BF16)
        for grp in range(n_grp):
            hn_ref[grp] = hn_t[:, grp * PEER_GROUP_TOKENS:(grp + 1) * PEER_GROUP_TOKENS]
        acc_ref[...] = jnp.zeros(acc_ref.shape, F32)
        for h in range(PEER_HEADS):
            r0 = h * 2 * PEER_HALF
            q1 = _dot(wq_ref[r0:r0 + PEER_HALF, :], hn_t)
            q2 = _dot(wq_ref[r0 + PEER_HALF:r0 + 2 * PEER_HALF, :], hn_t)
            s1 = _dot_f32(k1_ref[...], q1)
            s2 = _dot_f32(k2_ref[...], q2)
            ranked1 = _top_rows(s1, top1_ref, k)
            ranked2 = _top_rows(s2, top2_ref, k)
            v1 = top1_ref[...]
            v2 = top2_ref[...]
            brow = lax.broadcasted_iota(jnp.int32, (k, 1), 0)
            groups = []
            for a in range(k // 2):
                nb = k // (a + 1)
                rows_b = 8 if nb <= 8 else k
                blk = v1[a:a + 1, :] + v2[0:rows_b, :]
                groups.append(jnp.where(brow[0:rows_b] < nb, blk, -jnp.inf))
            groups.append(v1[k // 2:k, :] + v2[0:1, :])
            cand = jnp.concatenate(groups, axis=0)
            cand_ref[0:cand.shape[0], :] = cand
            _top_rows(cand_ref[0:cand.shape[0], :], sc_ref, k)
            sc = sc_ref[...]
            tau = sc[k - 1:k, :]
            z = jnp.sum(jnp.exp(sc - sc[0:1, :]), axis=0, keepdims=True)
            p1 = jnp.exp(s1 - v1[0:1, :])
            p2 = jnp.exp(s2 - v2[0:1, :]) / z
            cnt = jnp.zeros_like(s1)
            for a in range(k):
                cnt_a = jnp.sum(jnp.where(v1[a:a + 1, :] + v2 >= tau, 1.0, 0.0), axis=0, keepdims=True)
                cnt = jnp.where(ranked1 == (a + 1) * RANK_CODE, cnt_a, cnt)
            rank2 = jnp.where(ranked2 <= RANK_CODE, ranked2 * (1.0 / RANK_CODE) - 1.0, PEER_UNRANKED)
            for tc in range(n_tc):
                ts = slice(tc * LANES, (tc + 1) * LANES)
                cnt_ref[h, tc] = cnt[:, ts]
                p1_ref[h, tc] = p1[:, ts]
                rank2_ref[h, tc] = rank2[:, ts].astype(BF16).reshape(rank2_ref.shape[2:])
                p2_ref[h, tc] = p2[:, ts].astype(BF16).reshape(p2_ref.shape[2:])

    i0 = pl.multiple_of(e * n_i, n_i)

    packed = rank2_ref.shape[2:]
    def build_gates(tc):
        cnt_rows = [cnt_ref[h, tc, pl.ds(i0, n_i), :] for h in range(PEER_HEADS)]
        p1_rows = [p1_ref[h, tc, pl.ds(i0, n_i), :] for h in range(PEER_HEADS)]
        for ii0 in range(0, n_i, PEER_ROWS_PER_PASS):
            rows = range(ii0, ii0 + PEER_ROWS_PER_PASS)
            gsum = {ii: jnp.zeros(packed, BF16) for ii in rows}
            for h in range(PEER_HEADS):
                rank2 = rank2_ref[h, tc]
                p2 = p2_ref[h, tc]
                for ii in rows:
                    cnt_row = jnp.broadcast_to(cnt_rows[h][ii:ii + 1, :], packed[1:]).astype(BF16)
                    p1_row = jnp.broadcast_to(p1_rows[h][ii:ii + 1, :], packed[1:]).astype(BF16)
                    gsum[ii] = gsum[ii] + jnp.where(rank2 < cnt_row[None], p2 * p1_row[None],
                                                    jnp.zeros(packed, BF16))
            for ii in rows:
                g_ref[tc, ii * nk:(ii + 1) * nk, :] = gsum[ii].reshape(nk, LANES)

    cpg = PEER_GROUP_TOKENS // LANES
    d = vt_ref.shape[0]

    def group_body(grp, carry):
        hn_g = hn_ref[grp]
        for c in range(cpg):
            build_gates(grp * cpg + c)
        w_halves = []
        for rows in (slice(0, eb // 2), slice(eb // 2, eb)):
            act = _gelu(_dot(u_ref[rows, :], hn_g)).astype(BF16)
            w_halves.append(jnp.concatenate(
                [g_ref[grp * cpg + c, rows, :] * act[:, c * LANES:(c + 1) * LANES] for c in range(cpg)], axis=1))
        w = jnp.concatenate(w_halves, axis=0)
        for rows in (slice(0, d // 2), slice(d // 2, d)):
            acc_ref[grp, rows, :] += _dot(vt_ref[rows, :], w)
        return carry

    lax.fori_loop(0, n_grp, group_body, 0)

    @pl.when(e == n_e - 1)
    def _():
        for grp in range(n_grp):
            cols = slice(grp * PEER_GROUP_TOKENS, (grp + 1) * PEER_GROUP_TOKENS)
            o_ref[cols, :] = x_ref[cols, :] + acc_ref[grp].T


def _peer(x2d, gain, wq_t, keys1, keys2, u_b, vt_b, tt=512, eb=1024):
    t, d = x2d.shape
    n_exp = u_b.shape[0]
    const = lambda i, e: (0, 0)
    n_cand = 8 * 10
    n_tc = tt // LANES
    packed = (PEER_KEYS // BF16_SUBLANES, BF16_SUBLANES, LANES)
    return pl.pallas_call(
        _peer_kernel,
        grid=(t // tt, n_exp // eb),
        in_specs=[
            pl.BlockSpec((tt, d), lambda i, e: (i, 0)),
            pl.BlockSpec((1, d), const),
            pl.BlockSpec(wq_t.shape, const),
            pl.BlockSpec(keys1.shape, const),
            pl.BlockSpec(keys2.shape, const),
            pl.BlockSpec((eb, d), lambda i, e: (e, 0)),
            pl.BlockSpec((d, eb), lambda i, e: (0, e)),
        ],
        out_specs=pl.BlockSpec((tt, d), lambda i, e: (i, 0)),
        out_shape=jax.ShapeDtypeStruct((t, d), F32),
        scratch_shapes=[
            pltpu.VMEM((tt // PEER_GROUP_TOKENS, d, PEER_GROUP_TOKENS), BF16),
            pltpu.VMEM((PEER_HEADS, n_tc, PEER_KEYS, LANES), F32),
            pltpu.VMEM((PEER_HEADS, n_tc) + packed, BF16),
            pltpu.VMEM((PEER_HEADS, n_tc, PEER_KEYS, LANES), F32),
            pltpu.VMEM((PEER_HEADS, n_tc) + packed, BF16),
            pltpu.VMEM((PEER_TOPK, tt), F32),
            pltpu.VMEM((PEER_TOPK, tt), F32),
            pltpu.VMEM((n_cand, tt), F32),
            pltpu.VMEM((PEER_TOPK, tt), F32),
            pltpu.VMEM((n_tc, eb, LANES), BF16),
            pltpu.VMEM((tt // PEER_GROUP_TOKENS, d, PEER_GROUP_TOKENS), F32),
        ],
        compiler_params=_cparams("parallel", "arbitrary"),
        name="peer",
    )(x2d, gain, wq_t, keys1, keys2, u_b, vt_b)


def _pad_lanes(v, fill=0.0):
    return jnp.pad(v, ((0, 0), (0, LANES - v.shape[1])), constant_values=fill)


def _layer(x2d, mem2d, pos2d, bsz, seq, mem_len, norm_mix, w_in, conv_w, conv_b, dt_bias, a_log, d_skip,
           ssd_norm, dil_q_norm, dil_k_norm, mem_norm, w_mem_kv, mem_q_norm, mem_k_norm,
           w_up_ssd, w_up_dil, w_up_mem, w_out, norm_ffn, peer_w_q, peer_keys1, peer_keys2, peer_u, peer_v):
    d_inner = SSD_GROUPS * SSD_GROUP_WIDTH
    conv_dim = d_inner + 2 * SSD_GROUPS * SSD_STATE
    n_heads = SSD_GROUPS * SSD_HEADS_PER_GROUP
    dil_w = 3 * DIL_GROUP_WIDTH
    mem_w = MEM_HEADS * MEM_HEAD_DIM
    d = x2d.shape[1]

    o = 0
    parts = {}
    for name, width in (("z", d_inner), ("xbc", conv_dim), ("dt", n_heads), ("q_d", dil_w), ("k_d", dil_w),
                        ("v_d", dil_w), ("q_m", mem_w), ("gates", 3 * d)):
        parts[name] = w_in[:, o:o + width]
        o += width
    w_main = jnp.concatenate([parts[k] for k in ("xbc", "z", "q_d", "k_d", "v_d", "q_m", "gates")],
                             axis=1).astype(BF16)
    w_dt = _pad_lanes(parts["dt"])
    z_off = conv_dim
    qd_off = z_off + d_inner
    qm_off = qd_off + 3 * dil_w
    gate_off = qm_off + mem_w

    proj, dt_raw = _in_proj(x2d, norm_mix[None, :], w_main, w_dt)

    heads = jnp.arange(LANES)[:, None]
    chans = jnp.arange(d_inner)[None, :]
    expand = (chans // SSD_HEAD_DIM == heads).astype(BF16)
    y_ssd = _ssd(proj, dt_raw, conv_w, conv_b[None, :], _pad_lanes(dt_bias[None, :]), _pad_lanes(a_log[None, :]),
                 jnp.repeat(d_skip, SSD_HEAD_DIM)[None, :], ssd_norm[None, :], expand, bsz, seq)

    half = DIL_HEAD_DIM // 2
    inv_freq = ROPE_THETA ** (-jnp.arange(half, dtype=F32) / half)
    inv_freq = jnp.tile(inv_freq, LANES // half)[None, :]
    lane = jnp.arange(LANES)
    blockdiag = (lane[:, None] // DIL_HEAD_DIM == lane[None, :] // DIL_HEAD_DIM).astype(BF16)
    qkv = _dil_prep(proj, pos2d, jnp.tile(dil_q_norm, LANES // DIL_HEAD_DIM)[None, :],
                    jnp.tile(dil_k_norm, LANES // DIL_HEAD_DIM)[None, :], inv_freq, blockdiag,
                    qd_off // DIL_GROUP_WIDTH)
    y_dil = _dil_attn(qkv, bsz, seq)

    k_mem, v_mem = _mem_kv(mem2d, mem_norm[None, :], w_mem_kv.astype(BF16), mem_k_norm[None, :], bsz, mem_len)
    y_mem = _mem_attn(proj, k_mem, v_mem, mem_q_norm[None, :], bsz, seq, mem_len, qm_off // mem_w)

    x_mid = _combine(y_ssd, y_dil, y_mem, proj, x2d, w_up_ssd.astype(BF16), w_up_dil.astype(BF16),
                     w_up_mem.astype(BF16), w_out.astype(BF16), gate_off // (3 * d))

    return _peer(x_mid, norm_ffn[None, :], peer_w_q.T.astype(BF16), peer_keys1, peer_keys2,
                 peer_u.astype(BF16), peer_v.T.astype(BF16))


def kernel(x, mem, positions, norm_mix, w_in, conv_w, conv_b, dt_bias, a_log, d_skip, ssd_norm, dil_q_norm,
           dil_k_norm, mem_norm, w_mem_kv, mem_q_norm, mem_k_norm, w_up_ssd, w_up_dil, w_up_mem, w_out,
           norm_ffn, peer_w_q, peer_keys1, peer_keys2, peer_u, peer_v):
    bsz, seq, d = x.shape
    mem_len = mem.shape[1]
    x2d = x.reshape(bsz * seq, d)
    mem2d = mem.reshape(bsz * mem_len, d)
    pos2d = positions.reshape(bsz * seq, 1)
    for layer in range(norm_mix.shape[0]):
        x2d = _layer(x2d, mem2d, pos2d, bsz, seq, mem_len, norm_mix[layer], w_in[layer], conv_w[layer],
                     conv_b[layer], dt_bias[layer], a_log[layer], d_skip[layer], ssd_norm[layer],
                     dil_q_norm[layer], dil_k_norm[layer], mem_norm[layer], w_mem_kv[layer], mem_q_norm[layer],
                     mem_k_norm[layer], w_up_ssd[layer], w_up_dil[layer], w_up_mem[layer], w_out[layer],
                     norm_ffn[layer], peer_w_q[layer], peer_keys1[layer], peer_keys2[layer], peer_u[layer],
                     peer_v[layer])
    return x2d.reshape(bsz, seq, d)
```

```python
import functools
import math

import jax
import jax.numpy as jnp
from jax import lax
from jax.experimental import pallas as pl
from jax.experimental.pallas import tpu as pltpu

F32 = jnp.float32
BF16 = jnp.bfloat16
EPS = 1e-6
NEG = -1e30

VMEM_LIMIT_BYTES = 56 * 1024 * 1024
LANES = 128

SSD_CHUNK = 128
SSD_HEAD_DIM = 64
SSD_STATE = 128
SSD_GROUPS = 8
SSD_HEADS_PER_GROUP = 4
SSD_GROUP_WIDTH = SSD_HEADS_PER_GROUP * SSD_HEAD_DIM
SSD_CONV = 4
SSD_HISTORY_ROWS = 16
SSD_CONV_COLS = 512

DIL_DILATIONS = (1, 4, 16)
DIL_BLOCK = 128
DIL_SUPER = DIL_BLOCK * 16
DIL_HEAD_DIM = 64
DIL_GROUP_WIDTH = 8 * DIL_HEAD_DIM
ROPE_THETA = 10000.0

MEM_HEADS = 4
MEM_HEAD_DIM = 384

PEER_KEYS = 128
PEER_HEADS = 8
PEER_TOPK = 16
PEER_HALF = 128
PEER_UNRANKED = 255.0
BF16_SUBLANES = 16
PEER_ROWS_PER_PASS = 1
PEER_GROUP_TOKENS = 256


def _cparams(*sem):
    return pltpu.CompilerParams(dimension_semantics=sem, vmem_limit_bytes=VMEM_LIMIT_BYTES)


def _dot(a, b):
    return jnp.dot(a, b, preferred_element_type=F32)


def _dot_nt(a, b):
    return lax.dot_general(a, b, (((1,), (1,)), ((), ())), preferred_element_type=F32)


def _split2(x):
    hi = x.astype(BF16)
    lo = (x - hi.astype(F32)).astype(BF16)
    return hi, lo


def _split3(x):
    hi = x.astype(BF16)
    r = x - hi.astype(F32)
    mid = r.astype(BF16)
    lo = (r - mid.astype(F32)).astype(BF16)
    return hi, mid, lo


def _dot_f32(a, b):
    ah, al = _split2(a)
    bh, bl = _split2(b)
    return _dot(ah, bh) + (_dot(ah, bl) + _dot(al, bh))


def _dot_sel_right(m, e):
    hi, mid, lo = _split3(m)
    return _dot(hi, e) + (_dot(mid, e) + _dot(lo, e))


def _dot_sel_left(e, m):
    hi, mid, lo = _split3(m)
    return _dot(e, hi) + (_dot(e, mid) + _dot(e, lo))


def _sigmoid(x):
    return 1.0 / (1.0 + jnp.exp(-x))


def _silu(x):
    return x * _sigmoid(x)


def _in_proj_kernel(x_ref, g_ref, w_ref, wdt_ref, o_ref, dt_ref, h_ref):
    @pl.when(pl.program_id(1) == 0)
    def _():
        x = x_ref[...]
        ms = jnp.mean(x * x, axis=-1, keepdims=True)
        h = x * lax.rsqrt(ms + EPS) * g_ref[...]
        h_ref[...] = h.astype(BF16)
        dt_ref[...] = _dot_f32(h, wdt_ref[...])

    o_ref[...] = _dot(h_ref[...], w_ref[...]).astype(o_ref.dtype)


def _in_proj(x2d, gain, w_main, w_dt, tm=1024, tn=1536):
    t, d = x2d.shape
    n = w_main.shape[1]
    w_tiles = w_main.reshape(d, n // tn, tn).transpose(1, 0, 2)
    return pl.pallas_call(
        _in_proj_kernel,
        grid=(t // tm, n // tn),
        in_specs=[
            pl.BlockSpec((tm, d), lambda i, j: (i, 0)),
            pl.BlockSpec((1, d), lambda i, j: (0, 0)),
            pl.BlockSpec((None, d, tn), lambda i, j: (j, 0, 0)),
            pl.BlockSpec((d, LANES), lambda i, j: (0, 0)),
        ],
        out_specs=[
            pl.BlockSpec((tm, tn), lambda i, j: (i, j)),
            pl.BlockSpec((tm, LANES), lambda i, j: (i, 0)),
        ],
        out_shape=[jax.ShapeDtypeStruct((t, n), BF16), jax.ShapeDtypeStruct((t, LANES), F32)],
        scratch_shapes=[pltpu.VMEM((tm, d), BF16)],
        compiler_params=_cparams("parallel", "arbitrary"),
        name="in_proj",
    )(x2d, gain, w_tiles, w_dt)


def _ssd_kernel(xbc_ref, z_ref, dt_ref, cw_ref, cb_ref, dtb_ref, alog_ref, dskip_ref, gain_ref, e_ref, shift_ref,
                o_ref, xh_ref, u_ref, state_ref, y_ref):
    q = SSD_CHUNK
    d_inner = SSD_GROUPS * SSD_GROUP_WIDTH
    gn = SSD_GROUPS * SSD_STATE

    hist = SSD_HISTORY_ROWS

    @pl.when(pl.program_id(1) == 0)
    def _():
        xh_ref[0:hist, :] = jnp.zeros((hist, xh_ref.shape[1]), BF16)
        state_ref[...] = jnp.zeros(state_ref.shape, F32)

    xh_ref[hist:hist + q, :] = xbc_ref[...]
    for cblk in range(xh_ref.shape[1] // SSD_CONV_COLS):
        cs = slice(cblk * SSD_CONV_COLS, (cblk + 1) * SSD_CONV_COLS)
        delayed = _dot(shift_ref[...], xh_ref[:, cs])
        acc = cb_ref[:, cs] + cw_ref[3:4, cs] * xbc_ref[:, cs].astype(F32)
        for k in range(1, SSD_CONV):
            acc = acc + cw_ref[SSD_CONV - 1 - k:SSD_CONV - k, cs] * delayed[(k - 1) * q:k * q, :]
        u_ref[:, cs] = _silu(acc)
    xh_ref[0:hist, :] = xh_ref[q:q + hist, :]

    pre = dt_ref[...] + dtb_ref[...]
    dt = jnp.maximum(pre, 0.0) + jnp.log1p(jnp.exp(-jnp.abs(pre)))
    adt = dt * (-jnp.exp(alog_ref[...]))
    row = lax.broadcasted_iota(jnp.int32, (q, q), 0)
    col = lax.broadcasted_iota(jnp.int32, (q, q), 1)
    causal = row >= col
    tril = jnp.where(causal, 1.0, 0.0).astype(BF16)
    a_cs = _dot_sel_left(tril, adt)
    a_cs_t = a_cs.T
    dt_x = _dot(dt.astype(BF16), e_ref[...])
    acs_x = _dot_sel_right(a_cs, e_ref[...])
    ecs_x = jnp.exp(acs_x)
    dstate_x = jnp.exp(acs_x[q - 1:q, :] - acs_x)
    chunk_decay_x = ecs_x[q - 1:q, :]

    cbs, xds = [], []
    for g in range(SSD_GROUPS):
        c0 = g * SSD_GROUP_WIDTH
        xs_g = u_ref[:, c0:c0 + SSD_GROUP_WIDTH]
        b_g = u_ref[:, d_inner + g * SSD_STATE:d_inner + (g + 1) * SSD_STATE]
        c_g = u_ref[:, d_inner + gn + g * SSD_STATE:d_inner + gn + (g + 1) * SSD_STATE].astype(BF16)
        b_gt = b_g.T.astype(BF16)
        cbs.append(_dot(c_g, b_gt))
        xd = xs_g * dt_x[:, c0:c0 + SSD_GROUP_WIDTH]
        xds.append(xd.astype(BF16))
        xs_b = (xd * dstate_x[:, c0:c0 + SSD_GROUP_WIDTH]).astype(BF16)
        new_state = _dot(b_gt, xs_b)
        s_in = state_ref[g]
        y_off = _dot(c_g, s_in.astype(BF16)) * ecs_x[:, c0:c0 + SSD_GROUP_WIDTH]
        state_ref[g] = s_in * chunk_decay_x[:, c0:c0 + SSD_GROUP_WIDTH] + new_state
        y_ref[:, c0:c0 + SSD_GROUP_WIDTH] = y_off + dskip_ref[:, c0:c0 + SSD_GROUP_WIDTH] * xs_g

    masked = []
    for hd in range(SSD_GROUPS * SSD_HEADS_PER_GROUP):
        seg = a_cs[:, hd:hd + 1] - a_cs_t[hd:hd + 1, :]
        decay = jnp.where(causal, jnp.exp(jnp.where(causal, seg, 0.0)), 0.0)
        masked.append((cbs[hd // SSD_HEADS_PER_GROUP] * decay).astype(BF16))

    lane = lax.broadcasted_iota(jnp.int32, (1, SSD_GROUP_WIDTH), 1)
    head_ind = [jnp.where(lane // SSD_HEAD_DIM == r, 1.0, 0.0).astype(BF16) for r in range(SSD_HEADS_PER_GROUP)]
    for g in range(SSD_GROUPS):
        c0 = g * SSD_GROUP_WIDTH
        y_diag = None
        for r in range(SSD_HEADS_PER_GROUP):
            part = _dot(masked[g * SSD_HEADS_PER_GROUP + r], xds[g] * head_ind[r])
            y_diag = part if y_diag is None else y_diag + part
        y_ref[:, c0:c0 + SSD_GROUP_WIDTH] += y_diag

    yz = y_ref[...] * _silu(z_ref[...].astype(F32))
    ms = jnp.mean(yz * yz, axis=-1, keepdims=True)
    o_ref[...] = (yz * lax.rsqrt(ms + EPS) * gain_ref[...]).astype(o_ref.dtype)


def _ssd(proj, dt_raw, conv_w, conv_b, dt_bias, a_log, d_skip_x, gain, expand, bsz, seq):
    q = SSD_CHUNK
    nc = seq // q
    d_inner = SSD_GROUPS * SSD_GROUP_WIDTH
    conv_dim = conv_w.shape[1]
    const = lambda b, c: (0, 0)
    hist = SSD_HISTORY_ROWS
    out_row = jnp.arange((SSD_CONV - 1) * q)[:, None]
    src_col = jnp.arange(hist + q)[None, :]
    shift = (src_col == hist + out_row % q - (out_row // q + 1)).astype(BF16)
    return pl.pallas_call(
        _ssd_kernel,
        grid=(bsz, nc),
        in_specs=[
            pl.BlockSpec((q, conv_dim), lambda b, c: (b * nc + c, 0)),
            pl.BlockSpec((q, d_inner), lambda b, c: (b * nc + c, conv_dim // d_inner)),
            pl.BlockSpec((q, LANES), lambda b, c: (b * nc + c, 0)),
            pl.BlockSpec((SSD_CONV, conv_dim), const),
            pl.BlockSpec((1, conv_dim), const),
            pl.BlockSpec((1, LANES), const),
            pl.BlockSpec((1, LANES), const),
            pl.BlockSpec((1, d_inner), const),
            pl.BlockSpec((1, d_inner), const),
            pl.BlockSpec((LANES, d_inner), const),
            pl.BlockSpec(((SSD_CONV - 1) * q, hist + q), const),
        ],
        out_specs=pl.BlockSpec((q, d_inner), lambda b, c: (b * nc + c, 0)),
        out_shape=jax.ShapeDtypeStruct((bsz * seq, d_inner), BF16),
        scratch_shapes=[
            pltpu.VMEM((hist + q, conv_dim), BF16),
            pltpu.VMEM((q, conv_dim), F32),
            pltpu.VMEM((SSD_GROUPS, SSD_STATE, SSD_GROUP_WIDTH), F32),
            pltpu.VMEM((q, d_inner), F32),
        ],
        compiler_params=_cparams("parallel", "arbitrary"),
        name="ssd",
    )(proj, proj, dt_raw, conv_w, conv_b, dt_bias, a_log, d_skip_x, gain, expand, shift)


def _dil_prep_kernel(in_ref, pos_ref, qg_ref, kg_ref, invf_ref, bd_ref, o_ref, cos_ref, sin_ref, tmp_ref):
    kind = pl.program_id(1)
    g = pl.program_id(2)
    rows = in_ref.shape[0]

    @pl.when(jnp.logical_and(kind == 0, g == 0))
    def _():
        ang = pos_ref[...].astype(F32) * invf_ref[...]
        lane = lax.broadcasted_iota(jnp.int32, (1, LANES), 1)
        first_half = (lane % DIL_HEAD_DIM) < (DIL_HEAD_DIM // 2)
        cos_ref[...] = jnp.cos(ang)
        sin_ref[...] = jnp.sin(ang) * jnp.where(first_half, -1.0, 1.0)

    @pl.when(kind < 2)
    def _():
        gain = jnp.where(kind == 0, qg_ref[...] * (1.0 / math.sqrt(DIL_HEAD_DIM)), kg_ref[...])
        lane = lax.broadcasted_iota(jnp.int32, (1, LANES), 1)
        first_half = (lane % DIL_HEAD_DIM) < (DIL_HEAD_DIM // 2)
        for c in range(DIL_GROUP_WIDTH // LANES):
            xc = in_ref[:, c * LANES:(c + 1) * LANES].astype(F32)
            ss = _dot((xc * xc).astype(BF16), bd_ref[...])
            xn = xc * lax.rsqrt(ss * (1.0 / DIL_HEAD_DIM) + EPS) * gain
            partner = jnp.where(first_half, pltpu.roll(xn, LANES - DIL_HEAD_DIM // 2, 1),
                                pltpu.roll(xn, DIL_HEAD_DIM // 2, 1))
            tmp_ref[c] = xn * cos_ref[...] + partner * sin_ref[...]

    @pl.when(kind == 2)
    def _():
        for c in range(DIL_GROUP_WIDTH // LANES):
            tmp_ref[c] = in_ref[:, c * LANES:(c + 1) * LANES].astype(F32)

    @pl.when(g == 0)
    def _():
        for c in range(DIL_GROUP_WIDTH // LANES):
            o_ref[:, c * LANES:(c + 1) * LANES] = tmp_ref[c].astype(o_ref.dtype)

    for gi in (1, 2):
        d = DIL_DILATIONS[gi]

        @pl.when(g == gi)
        def _(d=d):
            for unit in range(rows // DIL_BLOCK):
                s, rho = unit // d, unit % d
                for c in range(DIL_GROUP_WIDTH // LANES):
                    src = tmp_ref[c, pl.ds(s * DIL_BLOCK * d + rho, DIL_BLOCK, stride=d), :]
                    o_ref[unit * DIL_BLOCK:(unit + 1) * DIL_BLOCK, c * LANES:(c + 1) * LANES] = src.astype(o_ref.dtype)


def _dil_prep(proj, pos2d, q_gain, k_gain, inv_freq, blockdiag, first_col_block):
    t = proj.shape[0]
    rows = DIL_SUPER
    w = DIL_GROUP_WIDTH
    const = lambda i, k, g: (0, 0)
    return pl.pallas_call(
        _dil_prep_kernel,
        grid=(t // rows, 3, 3),
        in_specs=[
            pl.BlockSpec((rows, w), lambda i, k, g: (i, first_col_block + k * 3 + g)),
            pl.BlockSpec((rows, 1), lambda i, k, g: (i, 0)),
            pl.BlockSpec((1, LANES), const),
            pl.BlockSpec((1, LANES), const),
            pl.BlockSpec((1, LANES), const),
            pl.BlockSpec((LANES, LANES), const),
        ],
        out_specs=pl.BlockSpec((None, rows, w), lambda i, k, g: (k * 3 + g, i, 0)),
        out_shape=jax.ShapeDtypeStruct((9, t, w), BF16),
        scratch_shapes=[pltpu.VMEM((rows, LANES), F32), pltpu.VMEM((rows, LANES), F32),
                        pltpu.VMEM((w // LANES, rows, LANES), F32)],
        compiler_params=_cparams("parallel", "arbitrary", "arbitrary"),
        name="dil_prep",
    )(proj, pos2d, q_gain, k_gain, inv_freq, blockdiag)


def _dil_attn_kernel(q_ref, kc_ref, kp_ref, vc_ref, vp_ref, o_ref, kbuf, vbuf, acc_ref, m_ref, l_ref):
    n = pl.program_id(1)
    g = pl.program_id(2)
    rows = q_ref.shape[0]
    blk = DIL_BLOCK
    n_units = rows // blk

    kbuf[0:rows, :] = kp_ref[...]
    kbuf[rows:2 * rows, :] = kc_ref[...]
    vbuf[0:rows, :] = vp_ref[...]
    vbuf[rows:2 * rows, :] = vc_ref[...]

    qi = lax.broadcasted_iota(jnp.int32, (blk, blk), 0)
    kj = lax.broadcasted_iota(jnp.int32, (blk, blk), 1)
    mask_cur = kj <= qi
    mask_prev = kj >= qi
    first_head = lax.broadcasted_iota(jnp.int32, (blk, LANES), 1) < DIL_HEAD_DIM
    head_ind = (jnp.where(first_head, 1.0, 0.0).astype(BF16), jnp.where(first_head, 0.0, 1.0).astype(BF16))

    def run_group(gi):
        d = DIL_DILATIONS[gi]

        def unit_body(u, carry):
            q_row = pl.multiple_of(u * blk, blk)
            c_row = pl.multiple_of(rows + u * blk, blk)
            p_row = pl.multiple_of(rows + u * blk - d * blk, blk)
            prev_bias = jnp.where(jnp.logical_or(n > 0, u >= d), 0.0, NEG)
            pos_row = (u // d) * (blk * d) + (u % d)
            pos_rows = pl.ds(pos_row, blk, stride=d) if d > 1 else pl.ds(pl.multiple_of(u * blk, blk), blk)
            n_hp = DIL_GROUP_WIDTH // LANES
            old = [] if gi == 0 else [(m_ref[hp, pos_rows, :], acc_ref[hp, pos_rows, :], l_ref[hp, pos_rows, :])
                                      for hp in range(n_hp)]
            n_heads = DIL_GROUP_WIDTH // DIL_HEAD_DIM
            head_lanes = [slice(h * DIL_HEAD_DIM, (h + 1) * DIL_HEAD_DIM) for h in range(n_heads)]
            scores = []
            for lanes in head_lanes:
                qh = q_ref[pl.ds(q_row, blk), lanes]
                s_cur = jnp.where(mask_cur, _dot_nt(qh, kbuf[pl.ds(c_row, blk), lanes]), NEG)
                s_prev = jnp.where(mask_prev, _dot_nt(qh, kbuf[pl.ds(p_row, blk), lanes]), NEG) + prev_bias
                scores.append((s_cur, s_prev))
            probs = []
            for s_cur, s_prev in scores:
                m_u = jnp.max(jnp.maximum(s_cur, s_prev), axis=1, keepdims=True)
                probs.append((jnp.exp(s_cur - m_u).astype(BF16), jnp.exp(s_prev - m_u).astype(BF16), m_u))
            new = []
            for hp in range(n_hp):
                pair = slice(hp * LANES, (hp + 1) * LANES)
                v_cur = vbuf[pl.ds(c_row, blk), pair]
                v_prev = vbuf[pl.ds(p_row, blk), pair]
                ol = None
                for hh, ind in enumerate(head_ind):
                    p_cur, p_prev, _ = probs[hp * 2 + hh]
                    rhs_cur = jnp.concatenate([v_cur * ind, ind], axis=1)
                    rhs_prev = jnp.concatenate([v_prev * ind, ind], axis=1)
                    part = _dot(p_cur, rhs_cur) + _dot(p_prev, rhs_prev)
                    ol = part if ol is None else ol + part
                m_pair = jnp.where(first_head, probs[hp * 2][2], probs[hp * 2 + 1][2])
                new.append((m_pair, ol[:, :LANES], ol[:, LANES:]))
            for hp in range(n_hp):
                m_u, o_u, l_u = new[hp]
                if gi == 0:
                    acc_ref[hp, pos_rows, :] = o_u
                    m_ref[hp, pos_rows, :] = m_u
                    l_ref[hp, pos_rows, :] = l_u
                else:
                    m_old, acc_old, l_old = old[hp]
                    m_new = jnp.maximum(m_old, m_u)
                    a_old = jnp.exp(m_old - m_new)
                    a_new = jnp.exp(m_u - m_new)
                    acc_ref[hp, pos_rows, :] = acc_old * a_old + o_u * a_new
                    l_ref[hp, pos_rows, :] = l_old * a_old + l_u * a_new
                    m_ref[hp, pos_rows, :] = m_new
            return carry

        lax.fori_loop(0, n_units, unit_body, 0)

    for gi in range(len(DIL_DILATIONS)):
        pl.when(g == gi)(functools.partial(run_group, gi))

    @pl.when(g == len(DIL_DILATIONS) - 1)
    def _():
        for hp in range(DIL_GROUP_WIDTH // LANES):
            o_ref[:, hp * LANES:(hp + 1) * LANES] = (acc_ref[hp] / l_ref[hp]).astype(o_ref.dtype)


def _dil_attn(qkv, bsz, seq):
    rows = DIL_SUPER
    w = DIL_GROUP_WIDTH
    nsb = seq // rows
    cur = lambda off: (lambda b, n, g: (off + g, b * nsb + n, 0))
    prev = lambda off: (lambda b, n, g: (off + g, b * nsb + jnp.maximum(n - 1, 0), 0))
    return pl.pallas_call(
        _dil_attn_kernel,
        grid=(bsz, nsb, 3),
        in_specs=[
            pl.BlockSpec((None, rows, w), cur(0)),
            pl.BlockSpec((None, rows, w), cur(3)),
            pl.BlockSpec((None, rows, w), prev(3)),
            pl.BlockSpec((None, rows, w), cur(6)),
            pl.BlockSpec((None, rows, w), prev(6)),
        ],
        out_specs=pl.BlockSpec((rows, w), lambda b, n, g: (b * nsb + n, 0)),
        out_shape=jax.ShapeDtypeStruct((bsz * seq, w), BF16),
        scratch_shapes=[pltpu.VMEM((2 * rows, w), BF16), pltpu.VMEM((2 * rows, w), BF16),
                        pltpu.VMEM((w // LANES, rows, LANES), F32), pltpu.VMEM((w // LANES, rows, LANES), F32),
                        pltpu.VMEM((w // LANES, rows, LANES), F32)],
        compiler_params=_cparams("parallel", "arbitrary", "arbitrary"),
        name="dil_attn",
    )(qkv, qkv, qkv, qkv, qkv)


def _mem_kv_kernel(mem_ref, g_ref, w_ref, kg_ref, k_ref, v_ref):
    x = mem_ref[...]
    ms = jnp.mean(x * x, axis=-1, keepdims=True)
    h = (x * lax.rsqrt(ms + EPS) * g_ref[...]).astype(BF16)
    kv = _dot(h, w_ref[...])
    width = MEM_HEADS * MEM_HEAD_DIM
    for hd in range(MEM_HEADS):
        sl = slice(hd * MEM_HEAD_DIM, (hd + 1) * MEM_HEAD_DIM)
        kh = kv[:, sl]
        ms = jnp.mean(kh * kh, axis=-1, keepdims=True)
        k_ref[:, sl] = (kh * lax.rsqrt(ms + EPS) * kg_ref[...]).astype(k_ref.dtype)
    v_ref[...] = kv[:, width:].astype(v_ref.dtype)


def _mem_kv(mem2d, gain, w_kv, k_gain, bsz, mem_len):
    d = mem2d.shape[1]
    width = MEM_HEADS * MEM_HEAD_DIM
    const = lambda b: (0, 0)
    return pl.pallas_call(
        _mem_kv_kernel,
        grid=(bsz,),
        in_specs=[
            pl.BlockSpec((mem_len, d), lambda b: (b, 0)),
            pl.BlockSpec((1, d), const),
            pl.BlockSpec((d, 2 * width), const),
            pl.BlockSpec((1, MEM_HEAD_DIM), const),
        ],
        out_specs=[pl.BlockSpec((mem_len, width), lambda b: (b, 0)),
                   pl.BlockSpec((mem_len, width), lambda b: (b, 0))],
        out_shape=[jax.ShapeDtypeStruct((bsz * mem_len, width), BF16)] * 2,
        compiler_params=_cparams("parallel"),
        name="mem_kv",
    )(mem2d, gain, w_kv, k_gain)


def _mem_attn_kernel(q_ref, k_ref, v_ref, qg_ref, o_ref):
    for hd in range(MEM_HEADS):
        sl = slice(hd * MEM_HEAD_DIM, (hd + 1) * MEM_HEAD_DIM)
        qh = q_ref[:, sl].astype(F32)
        ms = jnp.mean(qh * qh, axis=-1, keepdims=True)
        qn = (qh * lax.rsqrt(ms + EPS) * qg_ref[...]).astype(BF16)
        s = _dot_nt(qn, k_ref[:, sl]) * (1.0 / math.sqrt(MEM_HEAD_DIM))
        p = jnp.exp(s - jnp.max(s, axis=-1, keepdims=True))
        den = jnp.sum(p, axis=-1, keepdims=True)
        o_ref[:, sl] = (_dot(p.astype(BF16), v_ref[:, sl]) / den).astype(o_ref.dtype)


def _mem_attn(proj, k, v, q_gain, bsz, seq, mem_len, q_col_block, tq=512):
    width = MEM_HEADS * MEM_HEAD_DIM
    nq = seq // tq
    return pl.pallas_call(
        _mem_attn_kernel,
        grid=(bsz, nq),
        in_specs=[
            pl.BlockSpec((tq, width), lambda b, i: (b * nq + i, q_col_block)),
            pl.BlockSpec((mem_len, width), lambda b, i: (b, 0)),
            pl.BlockSpec((mem_len, width), lambda b, i: (b, 0)),
            pl.BlockSpec((1, MEM_HEAD_DIM), lambda b, i: (0, 0)),
        ],
        out_specs=pl.BlockSpec((tq, width), lambda b, i: (b * nq + i, 0)),
        out_shape=jax.ShapeDtypeStruct((bsz * seq, width), BF16),
        compiler_params=_cparams("parallel", "parallel"),
        name="mem_attn",
    )(proj, k, v, q_gain)


def _combine_kernel(ys_ref, yd_ref, ym_ref, gl_ref, x_ref, ws_ref, wd_ref, wm_ref, wo_ref, o_ref):
    d = x_ref.shape[1]
    gates = _sigmoid(gl_ref[...].astype(F32))
    merged = gates[:, 0:d] * _dot(ys_ref[...], ws_ref[...])
    merged = merged + gates[:, d:2 * d] * _dot(yd_ref[...], wd_ref[...])
    merged = merged + gates[:, 2 * d:3 * d] * _dot(ym_ref[...], wm_ref[...])
    o_ref[...] = x_ref[...] + _dot(merged.astype(BF16), wo_ref[...])


def _combine(y_ssd, y_dil, y_mem, proj, x2d, w_ssd, w_dil, w_mem, w_out, gate_col_block, tm=512):
    t, d = x2d.shape
    const = lambda i: (0, 0)
    row = lambda i: (i, 0)
    return pl.pallas_call(
        _combine_kernel,
        grid=(t // tm,),
        in_specs=[
            pl.BlockSpec((tm, y_ssd.shape[1]), row),
            pl.BlockSpec((tm, y_dil.shape[1]), row),
            pl.BlockSpec((tm, y_mem.shape[1]), row),
            pl.BlockSpec((tm, 3 * d), lambda i: (i, gate_col_block)),
            pl.BlockSpec((tm, d), row),
            pl.BlockSpec(w_ssd.shape, const),
            pl.BlockSpec(w_dil.shape, const),
            pl.BlockSpec(w_mem.shape, const),
            pl.BlockSpec(w_out.shape, const),
        ],
        out_specs=pl.BlockSpec((tm, d), row),
        out_shape=jax.ShapeDtypeStruct((t, d), F32),
        compiler_params=_cparams("parallel"),
        name="combine",
    )(y_ssd, y_dil, y_mem, proj, x2d, w_ssd, w_dil, w_mem, w_out)


RANK_CODE = -(2.0 ** 100)


def _top_rows(work, out_ref, count):
    for r in range(count):
        m = jnp.max(work, axis=0, keepdims=True)
        out_ref[r:r + 1, :] = m
        work = jnp.where(work == m, (r + 1) * RANK_CODE, work)
    return work


def _gelu(x):
    return 0.5 * x * (1.0 + lax.erf(x * (1.0 / math.sqrt(2.0))))


def _peer_kernel(x_ref, gain_ref, wq_ref, k1_ref, k2_ref, u_ref, vt_ref, o_ref,
                 hn_ref, cnt_ref, rank2_ref, p1_ref, p2_ref, top1_ref, top2_ref, cand_ref, sc_ref,
                 g_ref, acc_ref):
    e = pl.program_id(1)
    n_e = pl.num_programs(1)
    tt = x_ref.shape[0]
    eb = u_ref.shape[0]
    k = PEER_TOPK
    nk = PEER_KEYS
    n_tc = tt // LANES
    n_grp = tt // PEER_GROUP_TOKENS
    n_i = eb // nk
    assert n_i == 8, "one aligned 8-row group of key-1 rows per expert block"

    @pl.when(e == 0)
    def _():
        x = x_ref[...]
        ms = jnp.mean(x * x, axis=-1, keepdims=True)
        hn = x * lax.rsqrt(ms + EPS) * gain_ref[...]
        hn_t = hn.T.astype(BF16)
        for grp in range(n_grp):
            hn_ref[grp] = hn_t[:, grp * PEER_GROUP_TOKENS:(grp + 1) * PEER_GROUP_TOKENS]
        acc_ref[...] = jnp.zeros(acc_ref.shape, F32)
        for h in range(PEER_HEADS):
            r0 = h * 2 * PEER_HALF
            q1 = _dot(wq_ref[r0:r0 + PEER_HALF, :], hn_t)
            q2 = _dot(wq_ref[r0 + PEER_HALF:r0 + 2 * PEER_HALF, :], hn_t)
            s1 = _dot_f32(k1_ref[...], q1)
            s2 = _dot_f32(k2_ref[...], q2)
            ranked1 = _top_rows(s1, top1_ref, k)
            ranked2 = _top_rows(s2, top2_ref, k)
            v1 = top1_ref[...]
            v2 = top2_ref[...]
            brow = lax.broadcasted_iota(jnp.int32, (k, 1), 0)
            groups = []
            for a in range(k // 2):
                nb = k // (a + 1)
                rows_b = 8 if nb <= 8 else k
                blk = v1[a:a + 1, :] + v2[0:rows_b, :]
                groups.append(jnp.where(brow[0:rows_b] < nb, blk, -jnp.inf))
            groups.append(v1[k // 2:k, :] + v2[0:1, :])
            cand = jnp.concatenate(groups, axis=0)
            cand_ref[0:cand.shape[0], :] = cand
            _top_rows(cand_ref[0:cand.shape[0], :], sc_ref, k)
            sc = sc_ref[...]
            tau = sc[k - 1:k, :]
            z = jnp.sum(jnp.exp(sc - sc[0:1, :]), axis=0, keepdims=True)
            p1 = jnp.exp(s1 - v1[0:1, :])
            p2 = jnp.exp(s2 - v2[0:1, :]) / z
            cnt = jnp.zeros_like(s1)
            for a in range(k):
                cnt_a = jnp.sum(jnp.where(v1[a:a + 1, :] + v2 >= tau, 1.0, 0.0), axis=0, keepdims=True)
                cnt = jnp.where(ranked1 == (a + 1) * RANK_CODE, cnt_a, cnt)
            rank2 = jnp.where(ranked2 <= RANK_CODE, ranked2 * (1.0 / RANK_CODE) - 1.0, PEER_UNRANKED)
            for tc in range(n_tc):
                ts = slice(tc * LANES, (tc + 1) * LANES)
                cnt_ref[h, tc] = cnt[:, ts]
                p1_ref[h, tc] = p1[:, ts]
                rank2_ref[h, tc] = rank2[:, ts].astype(BF16).reshape(rank2_ref.shape[2:])
                p2_ref[h, tc] = p2[:, ts].astype(BF16).reshape(p2_ref.shape[2:])

    i0 = pl.multiple_of(e * n_i, n_i)

    packed = rank2_ref.shape[2:]
    def build_gates(tc):
        cnt_rows = [cnt_ref[h, tc, pl.ds(i0, n_i), :] for h in range(PEER_HEADS)]
        p1_rows = [p1_ref[h, tc, pl.ds(i0, n_i), :] for h in range(PEER_HEADS)]
        for ii0 in range(0, n_i, PEER_ROWS_PER_PASS):
            rows = range(ii0, ii0 + PEER_ROWS_PER_PASS)
            gsum = {ii: jnp.zeros(packed, BF16) for ii in rows}
            for h in range(PEER_HEADS):
                rank2 = rank2_ref[h, tc]
                p2 = p2_ref[h, tc]
                for ii in rows:
                    cnt_row = jnp.broadcast_to(cnt_rows[h][ii:ii + 1, :], packed[1:]).astype(BF16)
                    p1_row = jnp.broadcast_to(p1_rows[h][ii:ii + 1, :], packed[1:]).astype(BF16)
                    gsum[ii] = gsum[ii] + jnp.where(rank2 < cnt_row[None], p2 * p1_row[None],
                                                    jnp.zeros(packed, BF16))
            for ii in rows:
                g_ref[tc, ii * nk:(ii + 1) * nk, :] = gsum[ii].reshape(nk, LANES)

    cpg = PEER_GROUP_TOKENS // LANES
    d = vt_ref.shape[0]

    def group_body(grp, carry):
        hn_g = hn_ref[grp]
        for c in range(cpg):
            build_gates(grp * cpg + c)
        w_halves = []
        for rows in (slice(0, eb // 2), slice(eb // 2, eb)):
            act = _gelu(_dot(u_ref[rows, :], hn_g)).astype(BF16)
            w_halves.append(jnp.concatenate(
                [g_ref[grp * cpg + c, rows, :] * act[:, c * LANES:(c + 1) * LANES] for c in range(cpg)], axis=1))
        w = jnp.concatenate(w_halves, axis=0)
        for rows in (slice(0, d // 2), slice(d // 2, d)):
            acc_ref[grp, rows, :] += _dot(vt_ref[rows, :], w)
        return carry

    lax.fori_loop(0, n_grp, group_body, 0)

    @pl.when(e == n_e - 1)
    def _():
        for grp in range(n_grp):
            cols = slice(grp * PEER_GROUP_TOKENS, (grp + 1) * PEER_GROUP_TOKENS)
            o_ref[cols, :] = x_ref[cols, :] + acc_ref[grp].T


def _peer(x2d, gain, wq_t, keys1, keys2, u_b, v_b, tt=512, eb=1024):
    t, d = x2d.shape
    n_exp = u_b.shape[0]
    vt_b = v_b.reshape(n_exp // eb, eb, d).transpose(0, 2, 1)
    const = lambda i, e: (0, 0)
    n_cand = 8 * 10
    n_tc = tt // LANES
    packed = (PEER_KEYS // BF16_SUBLANES, BF16_SUBLANES, LANES)
    return pl.pallas_call(
        _peer_kernel,
        grid=(t // tt, n_exp // eb),
        in_specs=[
            pl.BlockSpec((tt, d), lambda i, e: (i, 0)),
            pl.BlockSpec((1, d), const),
            pl.BlockSpec(wq_t.shape, const),
            pl.BlockSpec(keys1.shape, const),
            pl.BlockSpec(keys2.shape, const),
            pl.BlockSpec((eb, d), lambda i, e: (e, 0)),
            pl.BlockSpec((None, d, eb), lambda i, e: (e, 0, 0)),
        ],
        out_specs=pl.BlockSpec((tt, d), lambda i, e: (i, 0)),
        out_shape=jax.ShapeDtypeStruct((t, d), F32),
        scratch_shapes=[
            pltpu.VMEM((tt // PEER_GROUP_TOKENS, d, PEER_GROUP_TOKENS), BF16),
            pltpu.VMEM((PEER_HEADS, n_tc, PEER_KEYS, LANES), F32),
            pltpu.VMEM((PEER_HEADS, n_tc) + packed, BF16),
            pltpu.VMEM((PEER_HEADS, n_tc, PEER_KEYS, LANES), F32),
            pltpu.VMEM((PEER_HEADS, n_tc) + packed, BF16),
            pltpu.VMEM((PEER_TOPK, tt), F32),
            pltpu.VMEM((PEER_TOPK, tt), F32),
            pltpu.VMEM((n_cand, tt), F32),
            pltpu.VMEM((PEER_TOPK, tt), F32),
            pltpu.VMEM((n_tc, eb, LANES), BF16),
            pltpu.VMEM((tt // PEER_GROUP_TOKENS, d, PEER_GROUP_TOKENS), F32),
        ],
        compiler_params=_cparams("parallel", "arbitrary"),
        name="peer",
    )(x2d, gain, wq_t, keys1, keys2, u_b, vt_b)


def _pad_lanes(v, fill=0.0):
    return jnp.pad(v, ((0, 0), (0, LANES - v.shape[1])), constant_values=fill)


def _layer(x2d, mem2d, pos2d, bsz, seq, mem_len, norm_mix, w_in, conv_w, conv_b, dt_bias, a_log, d_skip,
           ssd_norm, dil_q_norm, dil_k_norm, mem_norm, w_mem_kv, mem_q_norm, mem_k_norm,
           w_up_ssd, w_up_dil, w_up_mem, w_out, norm_ffn, peer_w_q, peer_keys1, peer_keys2, peer_u, peer_v):
    d_inner = SSD_GROUPS * SSD_GROUP_WIDTH
    conv_dim = d_inner + 2 * SSD_GROUPS * SSD_STATE
    n_heads = SSD_GROUPS * SSD_HEADS_PER_GROUP
    dil_w = 3 * DIL_GROUP_WIDTH
    mem_w = MEM_HEADS * MEM_HEAD_DIM
    d = x2d.shape[1]

    o = 0
    parts = {}
    for name, width in (("z", d_inner), ("xbc", conv_dim), ("dt", n_heads), ("q_d", dil_w), ("k_d", dil_w),
                        ("v_d", dil_w), ("q_m", mem_w), ("gates", 3 * d)):
        parts[name] = w_in[:, o:o + width]
        o += width
    w_main = jnp.concatenate([parts[k] for k in ("xbc", "z", "q_d", "k_d", "v_d", "q_m", "gates")],
                             axis=1).astype(BF16)
    w_dt = _pad_lanes(parts["dt"])
    z_off = conv_dim
    qd_off = z_off + d_inner
    qm_off = qd_off + 3 * dil_w
    gate_off = qm_off + mem_w

    proj, dt_raw = _in_proj(x2d, norm_mix[None, :], w_main, w_dt)

    heads = jnp.arange(LANES)[:, None]
    chans = jnp.arange(d_inner)[None, :]
    expand = (chans // SSD_HEAD_DIM == heads).astype(BF16)
    y_ssd = _ssd(proj, dt_raw, conv_w, conv_b[None, :], _pad_lanes(dt_bias[None, :]), _pad_lanes(a_log[None, :]),
                 jnp.repeat(d_skip, SSD_HEAD_DIM)[None, :], ssd_norm[None, :], expand, bsz, seq)

    half = DIL_HEAD_DIM // 2
    inv_freq = ROPE_THETA ** (-jnp.arange(half, dtype=F32) / half)
    inv_freq = jnp.tile(inv_freq, LANES // half)[None, :]
    lane = jnp.arange(LANES)
    blockdiag = (lane[:, None] // DIL_HEAD_DIM == lane[None, :] // DIL_HEAD_DIM).astype(BF16)
    qkv = _dil_prep(proj, pos2d, jnp.tile(dil_q_norm, LANES // DIL_HEAD_DIM)[None, :],
                    jnp.tile(dil_k_norm, LANES // DIL_HEAD_DIM)[None, :], inv_freq, blockdiag,
                    qd_off // DIL_GROUP_WIDTH)
    y_dil = _dil_attn(qkv, bsz, seq)

    k_mem, v_mem = _mem_kv(mem2d, mem_norm[None, :], w_mem_kv.astype(BF16), mem_k_norm[None, :], bsz, mem_len)
    y_mem = _mem_attn(proj, k_mem, v_mem, mem_q_norm[None, :], bsz, seq, mem_len, qm_off // mem_w)

    x_mid = _combine(y_ssd, y_dil, y_mem, proj, x2d, w_up_ssd.astype(BF16), w_up_dil.astype(BF16),
                     w_up_mem.astype(BF16), w_out.astype(BF16), gate_off // (3 * d))

    return _peer(x_mid, norm_ffn[None, :], peer_w_q.T.astype(BF16), peer_keys1, peer_keys2,
                 peer_u.astype(BF16), peer_v.astype(BF16))


def kernel(x, mem, positions, norm_mix, w_in, conv_w, conv_b, dt_bias, a_log, d_skip, ssd_norm, dil_q_norm,
           dil_k_norm, mem_norm, w_mem_kv, mem_q_norm, mem_k_norm, w_up_ssd, w_up_dil, w_up_mem, w_out,
           norm_ffn, peer_w_q, peer_keys1, peer_keys2, peer_u, peer_v):
    bsz, seq, d = x.shape
    mem_len = mem.shape[1]
    x2d = x.reshape(bsz * seq, d)
    mem2d = mem.reshape(bsz * mem_len, d)
    pos2d = positions.reshape(bsz * seq, 1)
    for layer in range(norm_mix.shape[0]):
        x2d = _layer(x2d, mem2d, pos2d, bsz, seq, mem_len, norm_mix[layer], w_in[layer], conv_w[layer],
                     conv_b[layer], dt_bias[layer], a_log[layer], d_skip[layer], ssd_norm[layer],
                     dil_q_norm[layer], dil_k_norm[layer], mem_norm[layer], w_mem_kv[layer], mem_q_norm[layer],
                     mem_k_norm[layer], w_up_ssd[layer], w_up_dil[layer], w_up_mem[layer], w_out[layer],
                     norm_ffn[layer], peer_w_q[layer], peer_keys1[layer], peer_keys2[layer], peer_u[layer],
                     peer_v[layer])
    return x2d.reshape(bsz, seq, d)
```

```python
import functools
import math

import jax
import jax.numpy as jnp
from jax import lax
from jax.experimental import pallas as pl
from jax.experimental.pallas import tpu as pltpu

F32 = jnp.float32
BF16 = jnp.bfloat16
EPS = 1e-6
NEG = -1e30

VMEM_LIMIT_BYTES = 56 * 1024 * 1024
LANES = 128

SSD_CHUNK = 128
SSD_HEAD_DIM = 64
SSD_STATE = 128
SSD_GROUPS = 8
SSD_HEADS_PER_GROUP = 4
SSD_GROUP_WIDTH = SSD_HEADS_PER_GROUP * SSD_HEAD_DIM
SSD_CONV = 4
SSD_HISTORY_ROWS = 16
SSD_CONV_COLS = 512

DIL_DILATIONS = (1, 4, 16)
DIL_BLOCK = 128
DIL_SUPER = DIL_BLOCK * 16
DIL_HEAD_DIM = 64
DIL_GROUP_WIDTH = 8 * DIL_HEAD_DIM
ROPE_THETA = 10000.0

MEM_HEADS = 4
MEM_HEAD_DIM = 384

PEER_KEYS = 128
PEER_HEADS = 8
PEER_TOPK = 16
PEER_HALF = 128
PEER_UNRANKED = 255.0
PEER_GROUP_TOKENS = 256


def _cparams(*sem):
    return pltpu.CompilerParams(dimension_semantics=sem, vmem_limit_bytes=VMEM_LIMIT_BYTES)


def _dot(a, b):
    return jnp.dot(a, b, preferred_element_type=F32)


def _dot_nt(a, b):
    return lax.dot_general(a, b, (((1,), (1,)), ((), ())), preferred_element_type=F32)


def _split2(x):
    hi = x.astype(BF16)
    lo = (x - hi.astype(F32)).astype(BF16)
    return hi, lo


def _split3(x):
    hi = x.astype(BF16)
    r = x - hi.astype(F32)
    mid = r.astype(BF16)
    lo = (r - mid.astype(F32)).astype(BF16)
    return hi, mid, lo


def _dot_f32(a, b):
    ah, al = _split2(a)
    bh, bl = _split2(b)
    return _dot(ah, bh) + (_dot(ah, bl) + _dot(al, bh))


def _dot_sel_right(m, e):
    hi, mid, lo = _split3(m)
    return _dot(hi, e) + (_dot(mid, e) + _dot(lo, e))


def _dot_sel_left(e, m):
    hi, mid, lo = _split3(m)
    return _dot(e, hi) + (_dot(e, mid) + _dot(e, lo))


def _sigmoid(x):
    return 1.0 / (1.0 + jnp.exp(-x))


def _silu(x):
    return x * _sigmoid(x)


def _in_proj_kernel(x_ref, g_ref, w_ref, wdt_ref, o_ref, dt_ref, h_ref):
    @pl.when(pl.program_id(1) == 0)
    def _():
        x = x_ref[...]
        ms = jnp.mean(x * x, axis=-1, keepdims=True)
        h = x * lax.rsqrt(ms + EPS) * g_ref[...]
        h_ref[...] = h.astype(BF16)
        dt_ref[...] = _dot_f32(h, wdt_ref[...])

    o_ref[...] = _dot(h_ref[...], w_ref[...]).astype(o_ref.dtype)


def _in_proj(x2d, gain, w_main, w_dt, tm=1024, tn=1536):
    t, d = x2d.shape
    n = w_main.shape[1]
    w_tiles = w_main.reshape(d, n // tn, tn).transpose(1, 0, 2)
    return pl.pallas_call(
        _in_proj_kernel,
        grid=(t // tm, n // tn),
        in_specs=[
            pl.BlockSpec((tm, d), lambda i, j: (i, 0)),
            pl.BlockSpec((1, d), lambda i, j: (0, 0)),
            pl.BlockSpec((None, d, tn), lambda i, j: (j, 0, 0)),
            pl.BlockSpec((d, LANES), lambda i, j: (0, 0)),
        ],
        out_specs=[
            pl.BlockSpec((tm, tn), lambda i, j: (i, j)),
            pl.BlockSpec((tm, LANES), lambda i, j: (i, 0)),
        ],
        out_shape=[jax.ShapeDtypeStruct((t, n), BF16), jax.ShapeDtypeStruct((t, LANES), F32)],
        scratch_shapes=[pltpu.VMEM((tm, d), BF16)],
        compiler_params=_cparams("parallel", "arbitrary"),
        name="in_proj",
    )(x2d, gain, w_tiles, w_dt)


def _ssd_kernel(xbc_ref, z_ref, dt_ref, cw_ref, cb_ref, dtb_ref, alog_ref, dskip_ref, gain_ref, e_ref, shift_ref,
                o_ref, xh_ref, u_ref, state_ref, y_ref):
    q = SSD_CHUNK
    d_inner = SSD_GROUPS * SSD_GROUP_WIDTH
    gn = SSD_GROUPS * SSD_STATE

    hist = SSD_HISTORY_ROWS

    @pl.when(pl.program_id(1) == 0)
    def _():
        xh_ref[0:hist, :] = jnp.zeros((hist, xh_ref.shape[1]), BF16)
        state_ref[...] = jnp.zeros(state_ref.shape, F32)

    xh_ref[hist:hist + q, :] = xbc_ref[...]
    for cblk in range(xh_ref.shape[1] // SSD_CONV_COLS):
        cs = slice(cblk * SSD_CONV_COLS, (cblk + 1) * SSD_CONV_COLS)
        delayed = _dot(shift_ref[...], xh_ref[:, cs])
        acc = cb_ref[:, cs] + cw_ref[3:4, cs] * xbc_ref[:, cs].astype(F32)
        for k in range(1, SSD_CONV):
            acc = acc + cw_ref[SSD_CONV - 1 - k:SSD_CONV - k, cs] * delayed[(k - 1) * q:k * q, :]
        u_ref[:, cs] = _silu(acc)
    xh_ref[0:hist, :] = xh_ref[q:q + hist, :]

    pre = dt_ref[...] + dtb_ref[...]
    dt = jnp.maximum(pre, 0.0) + jnp.log1p(jnp.exp(-jnp.abs(pre)))
    adt = dt * (-jnp.exp(alog_ref[...]))
    row = lax.broadcasted_iota(jnp.int32, (q, q), 0)
    col = lax.broadcasted_iota(jnp.int32, (q, q), 1)
    causal = row >= col
    tril = jnp.where(causal, 1.0, 0.0).astype(BF16)
    a_cs = _dot_sel_left(tril, adt)
    a_cs_t = a_cs.T
    dt_x = _dot(dt.astype(BF16), e_ref[...])
    acs_x = _dot_sel_right(a_cs, e_ref[...])
    ecs_x = jnp.exp(acs_x)
    dstate_x = jnp.exp(acs_x[q - 1:q, :] - acs_x)
    chunk_decay_x = ecs_x[q - 1:q, :]

    cbs, xds = [], []
    for g in range(SSD_GROUPS):
        c0 = g * SSD_GROUP_WIDTH
        xs_g = u_ref[:, c0:c0 + SSD_GROUP_WIDTH]
        b_g = u_ref[:, d_inner + g * SSD_STATE:d_inner + (g + 1) * SSD_STATE]
        c_g = u_ref[:, d_inner + gn + g * SSD_STATE:d_inner + gn + (g + 1) * SSD_STATE].astype(BF16)
        b_gt = b_g.T.astype(BF16)
        cbs.append(_dot(c_g, b_gt))
        xd = xs_g * dt_x[:, c0:c0 + SSD_GROUP_WIDTH]
        xds.append(xd.astype(BF16))
        xs_b = (xd * dstate_x[:, c0:c0 + SSD_GROUP_WIDTH]).astype(BF16)
        new_state = _dot(b_gt, xs_b)
        s_in = state_ref[g]
        y_off = _dot(c_g, s_in.astype(BF16)) * ecs_x[:, c0:c0 + SSD_GROUP_WIDTH]
        state_ref[g] = s_in * chunk_decay_x[:, c0:c0 + SSD_GROUP_WIDTH] + new_state
        y_ref[:, c0:c0 + SSD_GROUP_WIDTH] = y_off + dskip_ref[:, c0:c0 + SSD_GROUP_WIDTH] * xs_g

    masked = []
    for hd in range(SSD_GROUPS * SSD_HEADS_PER_GROUP):
        seg = a_cs[:, hd:hd + 1] - a_cs_t[hd:hd + 1, :]
        decay = jnp.where(causal, jnp.exp(jnp.where(causal, seg, 0.0)), 0.0)
        masked.append((cbs[hd // SSD_HEADS_PER_GROUP] * decay).astype(BF16))

    lane = lax.broadcasted_iota(jnp.int32, (1, SSD_GROUP_WIDTH), 1)
    head_ind = [jnp.where(lane // SSD_HEAD_DIM == r, 1.0, 0.0).astype(BF16) for r in range(SSD_HEADS_PER_GROUP)]
    for g in range(SSD_GROUPS):
        c0 = g * SSD_GROUP_WIDTH
        y_diag = None
        for r in range(SSD_HEADS_PER_GROUP):
            part = _dot(masked[g * SSD_HEADS_PER_GROUP + r], xds[g] * head_ind[r])
            y_diag = part if y_diag is None else y_diag + part
        y_ref[:, c0:c0 + SSD_GROUP_WIDTH] += y_diag

    yz = y_ref[...] * _silu(z_ref[...].astype(F32))
    ms = jnp.mean(yz * yz, axis=-1, keepdims=True)
    o_ref[...] = (yz * lax.rsqrt(ms + EPS) * gain_ref[...]).astype(o_ref.dtype)


def _ssd(proj, dt_raw, conv_w, conv_b, dt_bias, a_log, d_skip_x, gain, expand, bsz, seq):
    q = SSD_CHUNK
    nc = seq // q
    d_inner = SSD_GROUPS * SSD_GROUP_WIDTH
    conv_dim = conv_w.shape[1]
    const = lambda b, c: (0, 0)
    hist = SSD_HISTORY_ROWS
    out_row = jnp.arange((SSD_CONV - 1) * q)[:, None]
    src_col = jnp.arange(hist + q)[None, :]
    shift = (src_col == hist + out_row % q - (out_row // q + 1)).astype(BF16)
    return pl.pallas_call(
        _ssd_kernel,
        grid=(bsz, nc),
        in_specs=[
            pl.BlockSpec((q, conv_dim), lambda b, c: (b * nc + c, 0)),
            pl.BlockSpec((q, d_inner), lambda b, c: (b * nc + c, conv_dim // d_inner)),
            pl.BlockSpec((q, LANES), lambda b, c: (b * nc + c, 0)),
            pl.BlockSpec((SSD_CONV, conv_dim), const),
            pl.BlockSpec((1, conv_dim), const),
            pl.BlockSpec((1, LANES), const),
            pl.BlockSpec((1, LANES), const),
            pl.BlockSpec((1, d_inner), const),
            pl.BlockSpec((1, d_inner), const),
            pl.BlockSpec((LANES, d_inner), const),
            pl.BlockSpec(((SSD_CONV - 1) * q, hist + q), const),
        ],
        out_specs=pl.BlockSpec((q, d_inner), lambda b, c: (b * nc + c, 0)),
        out_shape=jax.ShapeDtypeStruct((bsz * seq, d_inner), BF16),
        scratch_shapes=[
            pltpu.VMEM((hist + q, conv_dim), BF16),
            pltpu.VMEM((q, conv_dim), F32),
            pltpu.VMEM((SSD_GROUPS, SSD_STATE, SSD_GROUP_WIDTH), F32),
            pltpu.VMEM((q, d_inner), F32),
        ],
        compiler_params=_cparams("parallel", "arbitrary"),
        name="ssd",
    )(proj, proj, dt_raw, conv_w, conv_b, dt_bias, a_log, d_skip_x, gain, expand, shift)


def _dil_prep_kernel(in_ref, pos_ref, qg_ref, kg_ref, invf_ref, bd_ref, o_ref, cos_ref, sin_ref, tmp_ref):
    kind = pl.program_id(1)
    g = pl.program_id(2)
    rows = in_ref.shape[0]

    @pl.when(jnp.logical_and(kind == 0, g == 0))
    def _():
        ang = pos_ref[...].astype(F32) * invf_ref[...]
        lane = lax.broadcasted_iota(jnp.int32, (1, LANES), 1)
        first_half = (lane % DIL_HEAD_DIM) < (DIL_HEAD_DIM // 2)
        cos_ref[...] = jnp.cos(ang)
        sin_ref[...] = jnp.sin(ang) * jnp.where(first_half, -1.0, 1.0)

    @pl.when(kind < 2)
    def _():
        gain = jnp.where(kind == 0, qg_ref[...] * (1.0 / math.sqrt(DIL_HEAD_DIM)), kg_ref[...])
        lane = lax.broadcasted_iota(jnp.int32, (1, LANES), 1)
        first_half = (lane % DIL_HEAD_DIM) < (DIL_HEAD_DIM // 2)
        for c in range(DIL_GROUP_WIDTH // LANES):
            xc = in_ref[:, c * LANES:(c + 1) * LANES].astype(F32)
            ss = _dot((xc * xc).astype(BF16), bd_ref[...])
            xn = xc * lax.rsqrt(ss * (1.0 / DIL_HEAD_DIM) + EPS) * gain
            partner = jnp.where(first_half, pltpu.roll(xn, LANES - DIL_HEAD_DIM // 2, 1),
                                pltpu.roll(xn, DIL_HEAD_DIM // 2, 1))
            tmp_ref[c] = xn * cos_ref[...] + partner * sin_ref[...]

    @pl.when(kind == 2)
    def _():
        for c in range(DIL_GROUP_WIDTH // LANES):
            tmp_ref[c] = in_ref[:, c * LANES:(c + 1) * LANES].astype(F32)

    @pl.when(g == 0)
    def _():
        for c in range(DIL_GROUP_WIDTH // LANES):
            o_ref[:, c * LANES:(c + 1) * LANES] = tmp_ref[c].astype(o_ref.dtype)

    for gi in (1, 2):
        d = DIL_DILATIONS[gi]

        @pl.when(g == gi)
        def _(d=d):
            for unit in range(rows // DIL_BLOCK):
                s, rho = unit // d, unit % d
                for c in range(DIL_GROUP_WIDTH // LANES):
                    src = tmp_ref[c, pl.ds(s * DIL_BLOCK * d + rho, DIL_BLOCK, stride=d), :]
                    o_ref[unit * DIL_BLOCK:(unit + 1) * DIL_BLOCK, c * LANES:(c + 1) * LANES] = src.astype(o_ref.dtype)


def _dil_prep(proj, pos2d, q_gain, k_gain, inv_freq, blockdiag, first_col_block):
    t = proj.shape[0]
    rows = DIL_SUPER
    w = DIL_GROUP_WIDTH
    const = lambda i, k, g: (0, 0)
    return pl.pallas_call(
        _dil_prep_kernel,
        grid=(t // rows, 3, 3),
        in_specs=[
            pl.BlockSpec((rows, w), lambda i, k, g: (i, first_col_block + k * 3 + g)),
            pl.BlockSpec((rows, 1), lambda i, k, g: (i, 0)),
            pl.BlockSpec((1, LANES), const),
            pl.BlockSpec((1, LANES), const),
            pl.BlockSpec((1, LANES), const),
            pl.BlockSpec((LANES, LANES), const),
        ],
        out_specs=pl.BlockSpec((None, rows, w), lambda i, k, g: (k * 3 + g, i, 0)),
        out_shape=jax.ShapeDtypeStruct((9, t, w), BF16),
        scratch_shapes=[pltpu.VMEM((rows, LANES), F32), pltpu.VMEM((rows, LANES), F32),
                        pltpu.VMEM((w // LANES, rows, LANES), F32)],
        compiler_params=_cparams("parallel", "arbitrary", "arbitrary"),
        name="dil_prep",
    )(proj, pos2d, q_gain, k_gain, inv_freq, blockdiag)


def _dil_attn_kernel(q_ref, kc_ref, kp_ref, vc_ref, vp_ref, o_ref, kbuf, vbuf, acc_ref, m_ref, l_ref):
    n = pl.program_id(1)
    g = pl.program_id(2)
    rows = q_ref.shape[0]
    blk = DIL_BLOCK
    n_units = rows // blk

    kbuf[0:rows, :] = kp_ref[...]
    kbuf[rows:2 * rows, :] = kc_ref[...]
    vbuf[0:rows, :] = vp_ref[...]
    vbuf[rows:2 * rows, :] = vc_ref[...]

    qi = lax.broadcasted_iota(jnp.int32, (blk, blk), 0)
    kj = lax.broadcasted_iota(jnp.int32, (blk, blk), 1)
    mask_cur = kj <= qi
    mask_prev = kj >= qi
    first_head = lax.broadcasted_iota(jnp.int32, (blk, LANES), 1) < DIL_HEAD_DIM
    head_ind = (jnp.where(first_head, 1.0, 0.0).astype(BF16), jnp.where(first_head, 0.0, 1.0).astype(BF16))

    def run_group(gi):
        d = DIL_DILATIONS[gi]

        def unit_body(u, carry):
            q_row = pl.multiple_of(u * blk, blk)
            c_row = pl.multiple_of(rows + u * blk, blk)
            p_row = pl.multiple_of(rows + u * blk - d * blk, blk)
            prev_bias = jnp.where(jnp.logical_or(n > 0, u >= d), 0.0, NEG)
            pos_row = (u // d) * (blk * d) + (u % d)
            pos_rows = pl.ds(pos_row, blk, stride=d) if d > 1 else pl.ds(pl.multiple_of(u * blk, blk), blk)
            n_hp = DIL_GROUP_WIDTH // LANES
            old = [] if gi == 0 else [(m_ref[hp, pos_rows, :], acc_ref[hp, pos_rows, :], l_ref[hp, pos_rows, :])
                                      for hp in range(n_hp)]
            n_heads = DIL_GROUP_WIDTH // DIL_HEAD_DIM
            head_lanes = [slice(h * DIL_HEAD_DIM, (h + 1) * DIL_HEAD_DIM) for h in range(n_heads)]
            scores = []
            for lanes in head_lanes:
                qh = q_ref[pl.ds(q_row, blk), lanes]
                s_cur = jnp.where(mask_cur, _dot_nt(qh, kbuf[pl.ds(c_row, blk), lanes]), NEG)
                s_prev = jnp.where(mask_prev, _dot_nt(qh, kbuf[pl.ds(p_row, blk), lanes]), NEG) + prev_bias
                scores.append((s_cur, s_prev))
            probs = []
            for s_cur, s_prev in scores:
                m_u = jnp.max(jnp.maximum(s_cur, s_prev), axis=1, keepdims=True)
                probs.append((jnp.exp(s_cur - m_u).astype(BF16), jnp.exp(s_prev - m_u).astype(BF16), m_u))
            new = []
            for hp in range(n_hp):
                pair = slice(hp * LANES, (hp + 1) * LANES)
                v_cur = vbuf[pl.ds(c_row, blk), pair]
                v_prev = vbuf[pl.ds(p_row, blk), pair]
                ol = None
                for hh, ind in enumerate(head_ind):
                    p_cur, p_prev, _ = probs[hp * 2 + hh]
                    rhs_cur = jnp.concatenate([v_cur * ind, ind], axis=1)
                    rhs_prev = jnp.concatenate([v_prev * ind, ind], axis=1)
                    part = _dot(p_cur, rhs_cur) + _dot(p_prev, rhs_prev)
                    ol = part if ol is None else ol + part
                m_pair = jnp.where(first_head, probs[hp * 2][2], probs[hp * 2 + 1][2])
                new.append((m_pair, ol[:, :LANES], ol[:, LANES:]))
            for hp in range(n_hp):
                m_u, o_u, l_u = new[hp]
                if gi == 0:
                    acc_ref[hp, pos_rows, :] = o_u
                    m_ref[hp, pos_rows, :] = m_u
                    l_ref[hp, pos_rows, :] = l_u
                else:
                    m_old, acc_old, l_old = old[hp]
                    m_new = jnp.maximum(m_old, m_u)
                    a_old = jnp.exp(m_old - m_new)
                    a_new = jnp.exp(m_u - m_new)
                    acc_ref[hp, pos_rows, :] = acc_old * a_old + o_u * a_new
                    l_ref[hp, pos_rows, :] = l_old * a_old + l_u * a_new
                    m_ref[hp, pos_rows, :] = m_new
            return carry

        lax.fori_loop(0, n_units, unit_body, 0)

    for gi in range(len(DIL_DILATIONS)):
        pl.when(g == gi)(functools.partial(run_group, gi))

    @pl.when(g == len(DIL_DILATIONS) - 1)
    def _():
        for hp in range(DIL_GROUP_WIDTH // LANES):
            o_ref[:, hp * LANES:(hp + 1) * LANES] = (acc_ref[hp] / l_ref[hp]).astype(o_ref.dtype)


def _dil_attn(qkv, bsz, seq):
    rows = DIL_SUPER
    w = DIL_GROUP_WIDTH
    nsb = seq // rows
    cur = lambda off: (lambda b, n, g: (off + g, b * nsb + n, 0))
    prev = lambda off: (lambda b, n, g: (off + g, b * nsb + jnp.maximum(n - 1, 0), 0))
    return pl.pallas_call(
        _dil_attn_kernel,
        grid=(bsz, nsb, 3),
        in_specs=[
            pl.BlockSpec((None, rows, w), cur(0)),
            pl.BlockSpec((None, rows, w), cur(3)),
            pl.BlockSpec((None, rows, w), prev(3)),
            pl.BlockSpec((None, rows, w), cur(6)),
            pl.BlockSpec((None, rows, w), prev(6)),
        ],
        out_specs=pl.BlockSpec((rows, w), lambda b, n, g: (b * nsb + n, 0)),
        out_shape=jax.ShapeDtypeStruct((bsz * seq, w), BF16),
        scratch_shapes=[pltpu.VMEM((2 * rows, w), BF16), pltpu.VMEM((2 * rows, w), BF16),
                        pltpu.VMEM((w // LANES, rows, LANES), F32), pltpu.VMEM((w // LANES, rows, LANES), F32),
                        pltpu.VMEM((w // LANES, rows, LANES), F32)],
        compiler_params=_cparams("parallel", "arbitrary", "arbitrary"),
        name="dil_attn",
    )(qkv, qkv, qkv, qkv, qkv)


def _mem_kv_kernel(mem_ref, g_ref, w_ref, kg_ref, k_ref, v_ref):
    x = mem_ref[...]
    ms = jnp.mean(x * x, axis=-1, keepdims=True)
    h = (x * lax.rsqrt(ms + EPS) * g_ref[...]).astype(BF16)
    kv = _dot(h, w_ref[...])
    width = MEM_HEADS * MEM_HEAD_DIM
    for hd in range(MEM_HEADS):
        sl = slice(hd * MEM_HEAD_DIM, (hd + 1) * MEM_HEAD_DIM)
        kh = kv[:, sl]
        ms = jnp.mean(kh * kh, axis=-1, keepdims=True)
        k_ref[:, sl] = (kh * lax.rsqrt(ms + EPS) * kg_ref[...]).astype(k_ref.dtype)
    v_ref[...] = kv[:, width:].astype(v_ref.dtype)


def _mem_kv(mem2d, gain, w_kv, k_gain, bsz, mem_len):
    d = mem2d.shape[1]
    width = MEM_HEADS * MEM_HEAD_DIM
    const = lambda b: (0, 0)
    return pl.pallas_call(
        _mem_kv_kernel,
        grid=(bsz,),
        in_specs=[
            pl.BlockSpec((mem_len, d), lambda b: (b, 0)),
            pl.BlockSpec((1, d), const),
            pl.BlockSpec((d, 2 * width), const),
            pl.BlockSpec((1, MEM_HEAD_DIM), const),
        ],
        out_specs=[pl.BlockSpec((mem_len, width), lambda b: (b, 0)),
                   pl.BlockSpec((mem_len, width), lambda b: (b, 0))],
        out_shape=[jax.ShapeDtypeStruct((bsz * mem_len, width), BF16)] * 2,
        compiler_params=_cparams("parallel"),
        name="mem_kv",
    )(mem2d, gain, w_kv, k_gain)


def _mem_attn_kernel(q_ref, k_ref, v_ref, qg_ref, o_ref):
    for hd in range(MEM_HEADS):
        sl = slice(hd * MEM_HEAD_DIM, (hd + 1) * MEM_HEAD_DIM)
        qh = q_ref[:, sl].astype(F32)
        ms = jnp.mean(qh * qh, axis=-1, keepdims=True)
        qn = (qh * lax.rsqrt(ms + EPS) * qg_ref[...]).astype(BF16)
        s = _dot_nt(qn, k_ref[:, sl]) * (1.0 / math.sqrt(MEM_HEAD_DIM))
        p = jnp.exp(s - jnp.max(s, axis=-1, keepdims=True))
        den = jnp.sum(p, axis=-1, keepdims=True)
        o_ref[:, sl] = (_dot(p.astype(BF16), v_ref[:, sl]) / den).astype(o_ref.dtype)


def _mem_attn(proj, k, v, q_gain, bsz, seq, mem_len, q_col_block, tq=512):
    width = MEM_HEADS * MEM_HEAD_DIM
    nq = seq // tq
    return pl.pallas_call(
        _mem_attn_kernel,
        grid=(bsz, nq),
        in_specs=[
            pl.BlockSpec((tq, width), lambda b, i: (b * nq + i, q_col_block)),
            pl.BlockSpec((mem_len, width), lambda b, i: (b, 0)),
            pl.BlockSpec((mem_len, width), lambda b, i: (b, 0)),
            pl.BlockSpec((1, MEM_HEAD_DIM), lambda b, i: (0, 0)),
        ],
        out_specs=pl.BlockSpec((tq, width), lambda b, i: (b * nq + i, 0)),
        out_shape=jax.ShapeDtypeStruct((bsz * seq, width), BF16),
        compiler_params=_cparams("parallel", "parallel"),
        name="mem_attn",
    )(proj, k, v, q_gain)


def _combine_kernel(ys_ref, yd_ref, ym_ref, gl_ref, x_ref, ws_ref, wd_ref, wm_ref, wo_ref, o_ref):
    d = x_ref.shape[1]
    gates = _sigmoid(gl_ref[...].astype(F32))
    merged = gates[:, 0:d] * _dot(ys_ref[...], ws_ref[...])
    merged = merged + gates[:, d:2 * d] * _dot(yd_ref[...], wd_ref[...])
    merged = merged + gates[:, 2 * d:3 * d] * _dot(ym_ref[...], wm_ref[...])
    o_ref[...] = x_ref[...] + _dot(merged.astype(BF16), wo_ref[...])


def _combine(y_ssd, y_dil, y_mem, proj, x2d, w_ssd, w_dil, w_mem, w_out, gate_col_block, tm=512):
    t, d = x2d.shape
    const = lambda i: (0, 0)
    row = lambda i: (i, 0)
    return pl.pallas_call(
        _combine_kernel,
        grid=(t // tm,),
        in_specs=[
            pl.BlockSpec((tm, y_ssd.shape[1]), row),
            pl.BlockSpec((tm, y_dil.shape[1]), row),
            pl.BlockSpec((tm, y_mem.shape[1]), row),
            pl.BlockSpec((tm, 3 * d), lambda i: (i, gate_col_block)),
            pl.BlockSpec((tm, d), row),
            pl.BlockSpec(w_ssd.shape, const),
            pl.BlockSpec(w_dil.shape, const),
            pl.BlockSpec(w_mem.shape, const),
            pl.BlockSpec(w_out.shape, const),
        ],
        out_specs=pl.BlockSpec((tm, d), row),
        out_shape=jax.ShapeDtypeStruct((t, d), F32),
        compiler_params=_cparams("parallel"),
        name="combine",
    )(y_ssd, y_dil, y_mem, proj, x2d, w_ssd, w_dil, w_mem, w_out)


RANK_CODE = -(2.0 ** 100)


def _top_rows(work, out_ref, count):
    for r in range(count):
        m = jnp.max(work, axis=0, keepdims=True)
        out_ref[r:r + 1, :] = m
        work = jnp.where(work == m, (r + 1) * RANK_CODE, work)
    return work


def _gelu(x):
    return 0.5 * x * (1.0 + lax.erf(x * (1.0 / math.sqrt(2.0))))


def _peer_kernel(x_ref, gain_ref, wq_ref, k1_ref, k2_ref, u_ref, vt_ref, o_ref,
                 hn_ref, cnt_ref, rank2_ref, p1_ref, p2_ref, top1_ref, top2_ref, cand_ref, sc_ref,
                 g_ref, acc_ref):
    e = pl.program_id(1)
    n_e = pl.num_programs(1)
    tt = x_ref.shape[0]
    eb = u_ref.shape[0]
    k = PEER_TOPK
    nk = PEER_KEYS
    n_tc = tt // LANES
    n_grp = tt // PEER_GROUP_TOKENS
    n_i = eb // nk
    assert n_i == 8, "one aligned 8-row group of key-1 rows per expert block"

    @pl.when(e == 0)
    def _():
        x = x_ref[...]
        ms = jnp.mean(x * x, axis=-1, keepdims=True)
        hn = x * lax.rsqrt(ms + EPS) * gain_ref[...]
        hn_t = hn.T.astype(BF16)
        for grp in range(n_grp):
            hn_ref[grp] = hn_t[:, grp * PEER_GROUP_TOKENS:(grp + 1) * PEER_GROUP_TOKENS]
        acc_ref[...] = jnp.zeros(acc_ref.shape, F32)
        for h in range(PEER_HEADS):
            r0 = h * 2 * PEER_HALF
            q1 = _dot(wq_ref[r0:r0 + PEER_HALF, :], hn_t)
            q2 = _dot(wq_ref[r0 + PEER_HALF:r0 + 2 * PEER_HALF, :], hn_t)
            s1 = _dot_f32(k1_ref[...], q1)
            s2 = _dot_f32(k2_ref[...], q2)
            ranked1 = _top_rows(s1, top1_ref, k)
            ranked2 = _top_rows(s2, top2_ref, k)
            v1 = top1_ref[...]
            v2 = top2_ref[...]
            brow = lax.broadcasted_iota(jnp.int32, (k, 1), 0)
            groups = []
            for a in range(k // 2):
                nb = k // (a + 1)
                rows_b = 8 if nb <= 8 else k
                blk = v1[a:a + 1, :] + v2[0:rows_b, :]
                groups.append(jnp.where(brow[0:rows_b] < nb, blk, -jnp.inf))
            groups.append(v1[k // 2:k, :] + v2[0:1, :])
            cand = jnp.concatenate(groups, axis=0)
            cand_ref[0:cand.shape[0], :] = cand
            _top_rows(cand_ref[0:cand.shape[0], :], sc_ref, k)
            sc = sc_ref[...]
            tau = sc[k - 1:k, :]
            z = jnp.sum(jnp.exp(sc - sc[0:1, :]), axis=0, keepdims=True)
            p1 = jnp.exp(s1 - v1[0:1, :])
            p2 = jnp.exp(s2 - v2[0:1, :]) / z
            cnt = jnp.zeros_like(s1)
            for a in range(k):
                cnt_a = jnp.sum(jnp.where(v1[a:a + 1, :] + v2 >= tau, 1.0, 0.0), axis=0, keepdims=True)
                cnt = jnp.where(ranked1 == (a + 1) * RANK_CODE, cnt_a, cnt)
            rank2 = jnp.where(ranked2 <= RANK_CODE, ranked2 * (1.0 / RANK_CODE) - 1.0, PEER_UNRANKED)
            for tc in range(n_tc):
                ts = slice(tc * LANES, (tc + 1) * LANES)
                cnt_ref[h, tc] = cnt[:, ts]
                p1_ref[h, tc] = p1[:, ts]
                rank2_ref[h, tc] = rank2[:, ts]
                p2_ref[h, tc] = p2[:, ts]

    i0 = pl.multiple_of(e * n_i, n_i)

    def gate_body(tc, carry):
        cnt_rows = [cnt_ref[h, tc, pl.ds(i0, n_i), :] for h in range(PEER_HEADS)]
        p1_rows = [p1_ref[h, tc, pl.ds(i0, n_i), :] for h in range(PEER_HEADS)]
        for ii in range(n_i):
            gsum = jnp.zeros((nk, LANES), F32)
            for h in range(PEER_HEADS):
                chosen = rank2_ref[h, tc] < cnt_rows[h][ii:ii + 1, :]
                gsum = gsum + jnp.where(chosen, p2_ref[h, tc] * p1_rows[h][ii:ii + 1, :], 0.0)
            g_ref[tc, ii * nk:(ii + 1) * nk, :] = gsum
        return carry

    lax.fori_loop(0, n_tc, gate_body, 0)

    cpg = PEER_GROUP_TOKENS // LANES
    d = vt_ref.shape[0]

    def group_body(grp, carry):
        hn_g = hn_ref[grp]
        w_halves = []
        for rows in (slice(0, eb // 2), slice(eb // 2, eb)):
            act = _gelu(_dot(u_ref[rows, :], hn_g))
            w_halves.append(jnp.concatenate(
                [(g_ref[grp * cpg + c, rows, :] * act[:, c * LANES:(c + 1) * LANES]).astype(BF16)
                 for c in range(cpg)], axis=1))
        w = jnp.concatenate(w_halves, axis=0)
        for rows in (slice(0, d // 2), slice(d // 2, d)):
            acc_ref[grp, rows, :] += _dot(vt_ref[rows, :], w)
        return carry

    lax.fori_loop(0, n_grp, group_body, 0)

    @pl.when(e == n_e - 1)
    def _():
        for grp in range(n_grp):
            cols = slice(grp * PEER_GROUP_TOKENS, (grp + 1) * PEER_GROUP_TOKENS)
            o_ref[cols, :] = x_ref[cols, :] + acc_ref[grp].T


def _peer(x2d, gain, wq_t, keys1, keys2, u_b, v_b, tt=512, eb=1024):
    t, d = x2d.shape
    n_exp = u_b.shape[0]
    vt_b = v_b.reshape(n_exp // eb, eb, d).transpose(0, 2, 1)
    const = lambda i, e: (0, 0)
    n_cand = 8 * 10
    n_tc = tt // LANES
    return pl.pallas_call(
        _peer_kernel,
        grid=(t // tt, n_exp // eb),
        in_specs=[
            pl.BlockSpec((tt, d), lambda i, e: (i, 0)),
            pl.BlockSpec((1, d), const),
            pl.BlockSpec(wq_t.shape, const),
            pl.BlockSpec(keys1.shape, const),
            pl.BlockSpec(keys2.shape, const),
            pl.BlockSpec((eb, d), lambda i, e: (e, 0)),
            pl.BlockSpec((None, d, eb), lambda i, e: (e, 0, 0)),
        ],
        out_specs=pl.BlockSpec((tt, d), lambda i, e: (i, 0)),
        out_shape=jax.ShapeDtypeStruct((t, d), F32),
        scratch_shapes=[
            pltpu.VMEM((tt // PEER_GROUP_TOKENS, d, PEER_GROUP_TOKENS), BF16),
            pltpu.VMEM((PEER_HEADS, n_tc, PEER_KEYS, LANES), F32),
            pltpu.VMEM((PEER_HEADS, n_tc, PEER_KEYS, LANES), F32),
            pltpu.VMEM((PEER_HEADS, n_tc, PEER_KEYS, LANES), F32),
            pltpu.VMEM((PEER_HEADS, n_tc, PEER_KEYS, LANES), F32),
            pltpu.VMEM((PEER_TOPK, tt), F32),
            pltpu.VMEM((PEER_TOPK, tt), F32),
            pltpu.VMEM((n_cand, tt), F32),
            pltpu.VMEM((PEER_TOPK, tt), F32),
            pltpu.VMEM((n_tc, eb, LANES), F32),
            pltpu.VMEM((tt // PEER_GROUP_TOKENS, d, PEER_GROUP_TOKENS), F32),
        ],
        compiler_params=_cparams("parallel", "arbitrary"),
        name="peer",
    )(x2d, gain, wq_t, keys1, keys2, u_b, vt_b)


def _pad_lanes(v, fill=0.0):
    return jnp.pad(v, ((0, 0), (0, LANES - v.shape[1])), constant_values=fill)


def _layer(x2d, mem2d, pos2d, bsz, seq, mem_len, norm_mix, w_in, conv_w, conv_b, dt_bias, a_log, d_skip,
           ssd_norm, dil_q_norm, dil_k_norm, mem_norm, w_mem_kv, mem_q_norm, mem_k_norm,
           w_up_ssd, w_up_dil, w_up_mem, w_out, norm_ffn, peer_w_q, peer_keys1, peer_keys2, peer_u, peer_v):
    d_inner = SSD_GROUPS * SSD_GROUP_WIDTH
    conv_dim = d_inner + 2 * SSD_GROUPS * SSD_STATE
    n_heads = SSD_GROUPS * SSD_HEADS_PER_GROUP
    dil_w = 3 * DIL_GROUP_WIDTH
    mem_w = MEM_HEADS * MEM_HEAD_DIM
    d = x2d.shape[1]

    o = 0
    parts = {}
    for name, width in (("z", d_inner), ("xbc", conv_dim), ("dt", n_heads), ("q_d", dil_w), ("k_d", dil_w),
                        ("v_d", dil_w), ("q_m", mem_w), ("gates", 3 * d)):
        parts[name] = w_in[:, o:o + width]
        o += width
    w_main = jnp.concatenate([parts[k] for k in ("xbc", "z", "q_d", "k_d", "v_d", "q_m", "gates")],
                             axis=1).astype(BF16)
    w_dt = _pad_lanes(parts["dt"])
    z_off = conv_dim
    qd_off = z_off + d_inner
    qm_off = qd_off + 3 * dil_w
    gate_off = qm_off + mem_w

    proj, dt_raw = _in_proj(x2d, norm_mix[None, :], w_main, w_dt)

    heads = jnp.arange(LANES)[:, None]
    chans = jnp.arange(d_inner)[None, :]
    expand = (chans // SSD_HEAD_DIM == heads).astype(BF16)
    y_ssd = _ssd(proj, dt_raw, conv_w, conv_b[None, :], _pad_lanes(dt_bias[None, :]), _pad_lanes(a_log[None, :]),
                 jnp.repeat(d_skip, SSD_HEAD_DIM)[None, :], ssd_norm[None, :], expand, bsz, seq)

    half = DIL_HEAD_DIM // 2
    inv_freq = ROPE_THETA ** (-jnp.arange(half, dtype=F32) / half)
    inv_freq = jnp.tile(inv_freq, LANES // half)[None, :]
    lane = jnp.arange(LANES)
    blockdiag = (lane[:, None] // DIL_HEAD_DIM == lane[None, :] // DIL_HEAD_DIM).astype(BF16)
    qkv = _dil_prep(proj, pos2d, jnp.tile(dil_q_norm, LANES // DIL_HEAD_DIM)[None, :],
                    jnp.tile(dil_k_norm, LANES // DIL_HEAD_DIM)[None, :], inv_freq, blockdiag,
                    qd_off // DIL_GROUP_WIDTH)
    y_dil = _dil_attn(qkv, bsz, seq)

    k_mem, v_mem = _mem_kv(mem2d, mem_norm[None, :], w_mem_kv.astype(BF16), mem_k_norm[None, :], bsz, mem_len)
    y_mem = _mem_attn(proj, k_mem, v_mem, mem_q_norm[None, :], bsz, seq, mem_len, qm_off // mem_w)

    x_mid = _combine(y_ssd, y_dil, y_mem, proj, x2d, w_up_ssd.astype(BF16), w_up_dil.astype(BF16),
                     w_up_mem.astype(BF16), w_out.astype(BF16), gate_off // (3 * d))

    return _peer(x_mid, norm_ffn[None, :], peer_w_q.T.astype(BF16), peer_keys1, peer_keys2,
                 peer_u.astype(BF16), peer_v.astype(BF16))


def kernel(x, mem, positions, norm_mix, w_in, conv_w, conv_b, dt_bias, a_log, d_skip, ssd_norm, dil_q_norm,
           dil_k_norm, mem_norm, w_mem_kv, mem_q_norm, mem_k_norm, w_up_ssd, w_up_dil, w_up_mem, w_out,
           norm_ffn, peer_w_q, peer_keys1, peer_keys2, peer_u, peer_v):
    bsz, seq, d = x.shape
    mem_len = mem.shape[1]
    x2d = x.reshape(bsz * seq, d)
    mem2d = mem.reshape(bsz * mem_len, d)
    pos2d = positions.reshape(bsz * seq, 1)
    for layer in range(norm_mix.shape[0]):
        x2d = _layer(x2d, mem2d, pos2d, bsz, seq, mem_len, norm_mix[layer], w_in[layer], conv_w[layer],
                     conv_b[layer], dt_bias[layer], a_log[layer], d_skip[layer], ssd_norm[layer],
                     dil_q_norm[layer], dil_k_norm[layer], mem_norm[layer], w_mem_kv[layer], mem_q_norm[layer],
                     mem_k_norm[layer], w_up_ssd[layer], w_up_dil[layer], w_up_mem[layer], w_out[layer],
                     norm_ffn[layer], peer_w_q[layer], peer_keys1[layer], peer_keys2[layer], peer_u[layer],
                     peer_v[layer])
    return x2d.reshape(bsz, seq, d)
```

```python
import functools
import math

import jax
import jax.numpy as jnp
from jax import lax
from jax.experimental import pallas as pl
from jax.experimental.pallas import tpu as pltpu

F32 = jnp.float32
BF16 = jnp.bfloat16
EPS = 1e-6
NEG = -1e30

VMEM_LIMIT_BYTES = 56 * 1024 * 1024
LANES = 128

SSD_CHUNK = 128
SSD_HEAD_DIM = 64
SSD_STATE = 128
SSD_GROUPS = 8
SSD_HEADS_PER_GROUP = 4
SSD_GROUP_WIDTH = SSD_HEADS_PER_GROUP * SSD_HEAD_DIM
SSD_CONV = 4
SSD_HISTORY_ROWS = 16
SSD_CONV_COLS = 512

DIL_DILATIONS = (1, 4, 16)
DIL_BLOCK = 128
DIL_SUPER = DIL_BLOCK * 16
DIL_HEAD_DIM = 64
DIL_GROUP_WIDTH = 8 * DIL_HEAD_DIM
ROPE_THETA = 10000.0

MEM_HEADS = 4
MEM_HEAD_DIM = 384

PEER_KEYS = 128
PEER_HEADS = 8
PEER_TOPK = 16
PEER_HALF = 128
PEER_UNRANKED = 255.0
PEER_GROUP_TOKENS = 256


def _cparams(*sem):
    return pltpu.CompilerParams(dimension_semantics=sem, vmem_limit_bytes=VMEM_LIMIT_BYTES)


def _dot(a, b):
    return jnp.dot(a, b, preferred_element_type=F32)


def _dot_nt(a, b):
    return lax.dot_general(a, b, (((1,), (1,)), ((), ())), preferred_element_type=F32)


def _split2(x):
    hi = x.astype(BF16)
    lo = (x - hi.astype(F32)).astype(BF16)
    return hi, lo


def _split3(x):
    hi = x.astype(BF16)
    r = x - hi.astype(F32)
    mid = r.astype(BF16)
    lo = (r - mid.astype(F32)).astype(BF16)
    return hi, mid, lo


def _dot_f32(a, b):
    ah, al = _split2(a)
    bh, bl = _split2(b)
    return _dot(ah, bh) + (_dot(ah, bl) + _dot(al, bh))


def _dot_sel_right(m, e):
    hi, mid, lo = _split3(m)
    return _dot(hi, e) + (_dot(mid, e) + _dot(lo, e))


def _dot_sel_left(e, m):
    hi, mid, lo = _split3(m)
    return _dot(e, hi) + (_dot(e, mid) + _dot(e, lo))


def _sigmoid(x):
    return 1.0 / (1.0 + jnp.exp(-x))


def _silu(x):
    return x * _sigmoid(x)


def _in_proj_kernel(x_ref, g_ref, w_ref, wdt_ref, o_ref, dt_ref, h_ref):
    @pl.when(pl.program_id(1) == 0)
    def _():
        x = x_ref[...]
        ms = jnp.mean(x * x, axis=-1, keepdims=True)
        h = x * lax.rsqrt(ms + EPS) * g_ref[...]
        h_ref[...] = h.astype(BF16)
        dt_ref[...] = _dot_f32(h, wdt_ref[...])

    o_ref[...] = _dot(h_ref[...], w_ref[...]).astype(o_ref.dtype)


def _in_proj(x2d, gain, w_main, w_dt, tm=1024, tn=1536):
    t, d = x2d.shape
    n = w_main.shape[1]
    w_tiles = w_main.reshape(d, n // tn, tn).transpose(1, 0, 2)
    return pl.pallas_call(
        _in_proj_kernel,
        grid=(t // tm, n // tn),
        in_specs=[
            pl.BlockSpec((tm, d), lambda i, j: (i, 0)),
            pl.BlockSpec((1, d), lambda i, j: (0, 0)),
            pl.BlockSpec((None, d, tn), lambda i, j: (j, 0, 0)),
            pl.BlockSpec((d, LANES), lambda i, j: (0, 0)),
        ],
        out_specs=[
            pl.BlockSpec((tm, tn), lambda i, j: (i, j)),
            pl.BlockSpec((tm, LANES), lambda i, j: (i, 0)),
        ],
        out_shape=[jax.ShapeDtypeStruct((t, n), BF16), jax.ShapeDtypeStruct((t, LANES), F32)],
        scratch_shapes=[pltpu.VMEM((tm, d), BF16)],
        compiler_params=_cparams("parallel", "arbitrary"),
        name="in_proj",
    )(x2d, gain, w_tiles, w_dt)


def _ssd_kernel(xbc_ref, z_ref, dt_ref, cw_ref, cb_ref, dtb_ref, alog_ref, dskip_ref, gain_ref, e_ref, shift_ref,
                o_ref, xh_ref, u_ref, state_ref, y_ref):
    q = SSD_CHUNK
    d_inner = SSD_GROUPS * SSD_GROUP_WIDTH
    gn = SSD_GROUPS * SSD_STATE

    hist = SSD_HISTORY_ROWS

    @pl.when(pl.program_id(1) == 0)
    def _():
        xh_ref[0:hist, :] = jnp.zeros((hist, xh_ref.shape[1]), BF16)
        state_ref[...] = jnp.zeros(state_ref.shape, F32)

    xh_ref[hist:hist + q, :] = xbc_ref[...]
    for cblk in range(xh_ref.shape[1] // SSD_CONV_COLS):
        cs = slice(cblk * SSD_CONV_COLS, (cblk + 1) * SSD_CONV_COLS)
        delayed = _dot(shift_ref[...], xh_ref[:, cs])
        acc = cb_ref[:, cs] + cw_ref[3:4, cs] * xbc_ref[:, cs].astype(F32)
        for k in range(1, SSD_CONV):
            acc = acc + cw_ref[SSD_CONV - 1 - k:SSD_CONV - k, cs] * delayed[(k - 1) * q:k * q, :]
        u_ref[:, cs] = _silu(acc)
    xh_ref[0:hist, :] = xh_ref[q:q + hist, :]

    pre = dt_ref[...] + dtb_ref[...]
    dt = jnp.maximum(pre, 0.0) + jnp.log1p(jnp.exp(-jnp.abs(pre)))
    adt = dt * (-jnp.exp(alog_ref[...]))
    row = lax.broadcasted_iota(jnp.int32, (q, q), 0)
    col = lax.broadcasted_iota(jnp.int32, (q, q), 1)
    causal = row >= col
    tril = jnp.where(causal, 1.0, 0.0).astype(BF16)
    a_cs = _dot_sel_left(tril, adt)
    a_cs_t = a_cs.T
    dt_x = _dot(dt.astype(BF16), e_ref[...])
    acs_x = _dot_sel_right(a_cs, e_ref[...])
    ecs_x = jnp.exp(acs_x)
    dstate_x = jnp.exp(acs_x[q - 1:q, :] - acs_x)
    chunk_decay_x = ecs_x[q - 1:q, :]

    cbs, xds = [], []
    for g in range(SSD_GROUPS):
        c0 = g * SSD_GROUP_WIDTH
        xs_g = u_ref[:, c0:c0 + SSD_GROUP_WIDTH]
        b_g = u_ref[:, d_inner + g * SSD_STATE:d_inner + (g + 1) * SSD_STATE]
        c_g = u_ref[:, d_inner + gn + g * SSD_STATE:d_inner + gn + (g + 1) * SSD_STATE].astype(BF16)
        b_gt = b_g.T.astype(BF16)
        cbs.append(_dot(c_g, b_gt))
        xd = xs_g * dt_x[:, c0:c0 + SSD_GROUP_WIDTH]
        xds.append(xd.astype(BF16))
        xs_b = (xd * dstate_x[:, c0:c0 + SSD_GROUP_WIDTH]).astype(BF16)
        new_state = _dot(b_gt, xs_b)
        s_in = state_ref[g]
        y_off = _dot(c_g, s_in.astype(BF16)) * ecs_x[:, c0:c0 + SSD_GROUP_WIDTH]
        state_ref[g] = s_in * chunk_decay_x[:, c0:c0 + SSD_GROUP_WIDTH] + new_state
        y_ref[:, c0:c0 + SSD_GROUP_WIDTH] = y_off + dskip_ref[:, c0:c0 + SSD_GROUP_WIDTH] * xs_g

    masked = []
    for hd in range(SSD_GROUPS * SSD_HEADS_PER_GROUP):
        seg = a_cs[:, hd:hd + 1] - a_cs_t[hd:hd + 1, :]
        decay = jnp.where(causal, jnp.exp(jnp.where(causal, seg, 0.0)), 0.0)
        masked.append((cbs[hd // SSD_HEADS_PER_GROUP] * decay).astype(BF16))

    lane = lax.broadcasted_iota(jnp.int32, (1, SSD_GROUP_WIDTH), 1)
    head_ind = [jnp.where(lane // SSD_HEAD_DIM == r, 1.0, 0.0).astype(BF16) for r in range(SSD_HEADS_PER_GROUP)]
    for g in range(SSD_GROUPS):
        c0 = g * SSD_GROUP_WIDTH
        y_diag = None
        for r in range(SSD_HEADS_PER_GROUP):
            part = _dot(masked[g * SSD_HEADS_PER_GROUP + r], xds[g] * head_ind[r])
            y_diag = part if y_diag is None else y_diag + part
        y_ref[:, c0:c0 + SSD_GROUP_WIDTH] += y_diag

    yz = y_ref[...] * _silu(z_ref[...].astype(F32))
    ms = jnp.mean(yz * yz, axis=-1, keepdims=True)
    o_ref[...] = (yz * lax.rsqrt(ms + EPS) * gain_ref[...]).astype(o_ref.dtype)


def _ssd(proj, dt_raw, conv_w, conv_b, dt_bias, a_log, d_skip_x, gain, expand, bsz, seq):
    q = SSD_CHUNK
    nc = seq // q
    d_inner = SSD_GROUPS * SSD_GROUP_WIDTH
    conv_dim = conv_w.shape[1]
    const = lambda b, c: (0, 0)
    hist = SSD_HISTORY_ROWS
    out_row = jnp.arange((SSD_CONV - 1) * q)[:, None]
    src_col = jnp.arange(hist + q)[None, :]
    shift = (src_col == hist + out_row % q - (out_row // q + 1)).astype(BF16)
    return pl.pallas_call(
        _ssd_kernel,
        grid=(bsz, nc),
        in_specs=[
            pl.BlockSpec((q, conv_dim), lambda b, c: (b * nc + c, 0)),
            pl.BlockSpec((q, d_inner), lambda b, c: (b * nc + c, conv_dim // d_inner)),
            pl.BlockSpec((q, LANES), lambda b, c: (b * nc + c, 0)),
            pl.BlockSpec((SSD_CONV, conv_dim), const),
            pl.BlockSpec((1, conv_dim), const),
            pl.BlockSpec((1, LANES), const),
            pl.BlockSpec((1, LANES), const),
            pl.BlockSpec((1, d_inner), const),
            pl.BlockSpec((1, d_inner), const),
            pl.BlockSpec((LANES, d_inner), const),
            pl.BlockSpec(((SSD_CONV - 1) * q, hist + q), const),
        ],
        out_specs=pl.BlockSpec((q, d_inner), lambda b, c: (b * nc + c, 0)),
        out_shape=jax.ShapeDtypeStruct((bsz * seq, d_inner), BF16),
        scratch_shapes=[
            pltpu.VMEM((hist + q, conv_dim), BF16),
            pltpu.VMEM((q, conv_dim), F32),
            pltpu.VMEM((SSD_GROUPS, SSD_STATE, SSD_GROUP_WIDTH), F32),
            pltpu.VMEM((q, d_inner), F32),
        ],
        compiler_params=_cparams("parallel", "arbitrary"),
        name="ssd",
    )(proj, proj, dt_raw, conv_w, conv_b, dt_bias, a_log, d_skip_x, gain, expand, shift)


def _dil_prep_kernel(in_ref, pos_ref, qg_ref, kg_ref, invf_ref, bd_ref, o_ref, cos_ref, sin_ref, tmp_ref):
    kind = pl.program_id(1)
    g = pl.program_id(2)
    rows = in_ref.shape[0]

    @pl.when(jnp.logical_and(kind == 0, g == 0))
    def _():
        ang = pos_ref[...].astype(F32) * invf_ref[...]
        lane = lax.broadcasted_iota(jnp.int32, (1, LANES), 1)
        first_half = (lane % DIL_HEAD_DIM) < (DIL_HEAD_DIM // 2)
        cos_ref[...] = jnp.cos(ang)
        sin_ref[...] = jnp.sin(ang) * jnp.where(first_half, -1.0, 1.0)

    @pl.when(kind < 2)
    def _():
        gain = jnp.where(kind == 0, qg_ref[...] * (1.0 / math.sqrt(DIL_HEAD_DIM)), kg_ref[...])
        lane = lax.broadcasted_iota(jnp.int32, (1, LANES), 1)
        first_half = (lane % DIL_HEAD_DIM) < (DIL_HEAD_DIM // 2)
        for c in range(DIL_GROUP_WIDTH // LANES):
            xc = in_ref[:, c * LANES:(c + 1) * LANES].astype(F32)
            ss = _dot((xc * xc).astype(BF16), bd_ref[...])
            xn = xc * lax.rsqrt(ss * (1.0 / DIL_HEAD_DIM) + EPS) * gain
            partner = jnp.where(first_half, pltpu.roll(xn, LANES - DIL_HEAD_DIM // 2, 1),
                                pltpu.roll(xn, DIL_HEAD_DIM // 2, 1))
            tmp_ref[c] = xn * cos_ref[...] + partner * sin_ref[...]

    @pl.when(kind == 2)
    def _():
        for c in range(DIL_GROUP_WIDTH // LANES):
            tmp_ref[c] = in_ref[:, c * LANES:(c + 1) * LANES].astype(F32)

    @pl.when(g == 0)
    def _():
        for c in range(DIL_GROUP_WIDTH // LANES):
            o_ref[:, c * LANES:(c + 1) * LANES] = tmp_ref[c].astype(o_ref.dtype)

    for gi in (1, 2):
        d = DIL_DILATIONS[gi]

        @pl.when(g == gi)
        def _(d=d):
            for unit in range(rows // DIL_BLOCK):
                s, rho = unit // d, unit % d
                for c in range(DIL_GROUP_WIDTH // LANES):
                    src = tmp_ref[c, pl.ds(s * DIL_BLOCK * d + rho, DIL_BLOCK, stride=d), :]
                    o_ref[unit * DIL_BLOCK:(unit + 1) * DIL_BLOCK, c * LANES:(c + 1) * LANES] = src.astype(o_ref.dtype)


def _dil_prep(proj, pos2d, q_gain, k_gain, inv_freq, blockdiag, first_col_block):
    t = proj.shape[0]
    rows = DIL_SUPER
    w = DIL_GROUP_WIDTH
    const = lambda i, k, g: (0, 0)
    return pl.pallas_call(
        _dil_prep_kernel,
        grid=(t // rows, 3, 3),
        in_specs=[
            pl.BlockSpec((rows, w), lambda i, k, g: (i, first_col_block + k * 3 + g)),
            pl.BlockSpec((rows, 1), lambda i, k, g: (i, 0)),
            pl.BlockSpec((1, LANES), const),
            pl.BlockSpec((1, LANES), const),
            pl.BlockSpec((1, LANES), const),
            pl.BlockSpec((LANES, LANES), const),
        ],
        out_specs=pl.BlockSpec((None, rows, w), lambda i, k, g: (k * 3 + g, i, 0)),
        out_shape=jax.ShapeDtypeStruct((9, t, w), BF16),
        scratch_shapes=[pltpu.VMEM((rows, LANES), F32), pltpu.VMEM((rows, LANES), F32),
                        pltpu.VMEM((w // LANES, rows, LANES), F32)],
        compiler_params=_cparams("parallel", "arbitrary", "arbitrary"),
        name="dil_prep",
    )(proj, pos2d, q_gain, k_gain, inv_freq, blockdiag)


def _dil_attn_kernel(q_ref, kc_ref, kp_ref, vc_ref, vp_ref, o_ref, kbuf, vbuf, acc_ref, m_ref, l_ref):
    n = pl.program_id(1)
    g = pl.program_id(2)
    rows = q_ref.shape[0]
    blk = DIL_BLOCK
    n_units = rows // blk

    kbuf[0:rows, :] = kp_ref[...]
    kbuf[rows:2 * rows, :] = kc_ref[...]
    vbuf[0:rows, :] = vp_ref[...]
    vbuf[rows:2 * rows, :] = vc_ref[...]

    qi = lax.broadcasted_iota(jnp.int32, (blk, blk), 0)
    kj = lax.broadcasted_iota(jnp.int32, (blk, blk), 1)
    mask_cur = kj <= qi
    mask_prev = kj >= qi
    first_head = lax.broadcasted_iota(jnp.int32, (blk, LANES), 1) < DIL_HEAD_DIM
    head_ind = (jnp.where(first_head, 1.0, 0.0).astype(BF16), jnp.where(first_head, 0.0, 1.0).astype(BF16))

    def run_group(gi):
        d = DIL_DILATIONS[gi]

        def unit_body(u, carry):
            q_row = pl.multiple_of(u * blk, blk)
            c_row = pl.multiple_of(rows + u * blk, blk)
            p_row = pl.multiple_of(rows + u * blk - d * blk, blk)
            prev_bias = jnp.where(jnp.logical_or(n > 0, u >= d), 0.0, NEG)
            pos_row = (u // d) * (blk * d) + (u % d)
            pos_rows = pl.ds(pos_row, blk, stride=d) if d > 1 else pl.ds(pl.multiple_of(u * blk, blk), blk)
            n_hp = DIL_GROUP_WIDTH // LANES
            old = [] if gi == 0 else [(m_ref[hp, pos_rows, :], acc_ref[hp, pos_rows, :], l_ref[hp, pos_rows, :])
                                      for hp in range(n_hp)]
            n_heads = DIL_GROUP_WIDTH // DIL_HEAD_DIM
            head_lanes = [slice(h * DIL_HEAD_DIM, (h + 1) * DIL_HEAD_DIM) for h in range(n_heads)]
            scores = []
            for lanes in head_lanes:
                qh = q_ref[pl.ds(q_row, blk), lanes]
                s_cur = jnp.where(mask_cur, _dot_nt(qh, kbuf[pl.ds(c_row, blk), lanes]), NEG)
                s_prev = jnp.where(mask_prev, _dot_nt(qh, kbuf[pl.ds(p_row, blk), lanes]), NEG) + prev_bias
                scores.append((s_cur, s_prev))
            probs = []
            for s_cur, s_prev in scores:
                m_u = jnp.max(jnp.maximum(s_cur, s_prev), axis=1, keepdims=True)
                probs.append((jnp.exp(s_cur - m_u).astype(BF16), jnp.exp(s_prev - m_u).astype(BF16), m_u))
            new = []
            for hp in range(n_hp):
                pair = slice(hp * LANES, (hp + 1) * LANES)
                v_cur = vbuf[pl.ds(c_row, blk), pair]
                v_prev = vbuf[pl.ds(p_row, blk), pair]
                ol = None
                for hh, ind in enumerate(head_ind):
                    p_cur, p_prev, _ = probs[hp * 2 + hh]
                    rhs_cur = jnp.concatenate([v_cur * ind, ind], axis=1)
                    rhs_prev = jnp.concatenate([v_prev * ind, ind], axis=1)
                    part = _dot(p_cur, rhs_cur) + _dot(p_prev, rhs_prev)
                    ol = part if ol is None else ol + part
                m_pair = jnp.where(first_head, probs[hp * 2][2], probs[hp * 2 + 1][2])
                new.append((m_pair, ol[:, :LANES], ol[:, LANES:]))
            for hp in range(n_hp):
                m_u, o_u, l_u = new[hp]
                if gi == 0:
                    acc_ref[hp, pos_rows, :] = o_u
                    m_ref[hp, pos_rows, :] = m_u
                    l_ref[hp, pos_rows, :] = l_u
                else:
                    m_old, acc_old, l_old = old[hp]
                    m_new = jnp.maximum(m_old, m_u)
                    a_old = jnp.exp(m_old - m_new)
                    a_new = jnp.exp(m_u - m_new)
                    acc_ref[hp, pos_rows, :] = acc_old * a_old + o_u * a_new
                    l_ref[hp, pos_rows, :] = l_old * a_old + l_u * a_new
                    m_ref[hp, pos_rows, :] = m_new
            return carry

        lax.fori_loop(0, n_units, unit_body, 0)

    for gi in range(len(DIL_DILATIONS)):
        pl.when(g == gi)(functools.partial(run_group, gi))

    @pl.when(g == len(DIL_DILATIONS) - 1)
    def _():
        for hp in range(DIL_GROUP_WIDTH // LANES):
            o_ref[:, hp * LANES:(hp + 1) * LANES] = (acc_ref[hp] / l_ref[hp]).astype(o_ref.dtype)


def _dil_attn(qkv, bsz, seq):
    rows = DIL_SUPER
    w = DIL_GROUP_WIDTH
    nsb = seq // rows
    cur = lambda off: (lambda b, n, g: (off + g, b * nsb + n, 0))
    prev = lambda off: (lambda b, n, g: (off + g, b * nsb + jnp.maximum(n - 1, 0), 0))
    return pl.pallas_call(
        _dil_attn_kernel,
        grid=(bsz, nsb, 3),
        in_specs=[
            pl.BlockSpec((None, rows, w), cur(0)),
            pl.BlockSpec((None, rows, w), cur(3)),
            pl.BlockSpec((None, rows, w), prev(3)),
            pl.BlockSpec((None, rows, w), cur(6)),
            pl.BlockSpec((None, rows, w), prev(6)),
        ],
        out_specs=pl.BlockSpec((rows, w), lambda b, n, g: (b * nsb + n, 0)),
        out_shape=jax.ShapeDtypeStruct((bsz * seq, w), BF16),
        scratch_shapes=[pltpu.VMEM((2 * rows, w), BF16), pltpu.VMEM((2 * rows, w), BF16),
                        pltpu.VMEM((w // LANES, rows, LANES), F32), pltpu.VMEM((w // LANES, rows, LANES), F32),
                        pltpu.VMEM((w // LANES, rows, LANES), F32)],
        compiler_params=_cparams("parallel", "arbitrary", "arbitrary"),
        name="dil_attn",
    )(qkv, qkv, qkv, qkv, qkv)


def _mem_kv_kernel(mem_ref, g_ref, w_ref, kg_ref, k_ref, v_ref):
    x = mem_ref[...]
    ms = jnp.mean(x * x, axis=-1, keepdims=True)
    h = (x * lax.rsqrt(ms + EPS) * g_ref[...]).astype(BF16)
    kv = _dot(h, w_ref[...])
    width = MEM_HEADS * MEM_HEAD_DIM
    for hd in range(MEM_HEADS):
        sl = slice(hd * MEM_HEAD_DIM, (hd + 1) * MEM_HEAD_DIM)
        kh = kv[:, sl]
        ms = jnp.mean(kh * kh, axis=-1, keepdims=True)
        k_ref[:, sl] = (kh * lax.rsqrt(ms + EPS) * kg_ref[...]).astype(k_ref.dtype)
    v_ref[...] = kv[:, width:].astype(v_ref.dtype)


def _mem_kv(mem2d, gain, w_kv, k_gain, bsz, mem_len):
    d = mem2d.shape[1]
    width = MEM_HEADS * MEM_HEAD_DIM
    const = lambda b: (0, 0)
    return pl.pallas_call(
        _mem_kv_kernel,
        grid=(bsz,),
        in_specs=[
            pl.BlockSpec((mem_len, d), lambda b: (b, 0)),
            pl.BlockSpec((1, d), const),
            pl.BlockSpec((d, 2 * width), const),
            pl.BlockSpec((1, MEM_HEAD_DIM), const),
        ],
        out_specs=[pl.BlockSpec((mem_len, width), lambda b: (b, 0)),
                   pl.BlockSpec((mem_len, width), lambda b: (b, 0))],
        out_shape=[jax.ShapeDtypeStruct((bsz * mem_len, width), BF16)] * 2,
        compiler_params=_cparams("parallel"),
        name="mem_kv",
    )(mem2d, gain, w_kv, k_gain)


def _mem_attn_kernel(q_ref, k_ref, v_ref, qg_ref, o_ref):
    for hd in range(MEM_HEADS):
        sl = slice(hd * MEM_HEAD_DIM, (hd + 1) * MEM_HEAD_DIM)
        qh = q_ref[:, sl].astype(F32)
        ms = jnp.mean(qh * qh, axis=-1, keepdims=True)
        qn = (qh * lax.rsqrt(ms + EPS) * qg_ref[...]).astype(BF16)
        s = _dot_nt(qn, k_ref[:, sl]) * (1.0 / math.sqrt(MEM_HEAD_DIM))
        p = jnp.exp(s - jnp.max(s, axis=-1, keepdims=True))
        den = jnp.sum(p, axis=-1, keepdims=True)
        o_ref[:, sl] = (_dot(p.astype(BF16), v_ref[:, sl]) / den).astype(o_ref.dtype)


def _mem_attn(proj, k, v, q_gain, bsz, seq, mem_len, q_col_block, tq=512):
    width = MEM_HEADS * MEM_HEAD_DIM
    nq = seq // tq
    return pl.pallas_call(
        _mem_attn_kernel,
        grid=(bsz, nq),
        in_specs=[
            pl.BlockSpec((tq, width), lambda b, i: (b * nq + i, q_col_block)),
            pl.BlockSpec((mem_len, width), lambda b, i: (b, 0)),
            pl.BlockSpec((mem_len, width), lambda b, i: (b, 0)),
            pl.BlockSpec((1, MEM_HEAD_DIM), lambda b, i: (0, 0)),
        ],
        out_specs=pl.BlockSpec((tq, width), lambda b, i: (b * nq + i, 0)),
        out_shape=jax.ShapeDtypeStruct((bsz * seq, width), BF16),
        compiler_params=_cparams("parallel", "parallel"),
        name="mem_attn",
    )(proj, k, v, q_gain)


def _combine_kernel(ys_ref, yd_ref, ym_ref, gl_ref, x_ref, ws_ref, wd_ref, wm_ref, wo_ref, o_ref):
    d = x_ref.shape[1]
    gates = _sigmoid(gl_ref[...].astype(F32))
    merged = gates[:, 0:d] * _dot(ys_ref[...], ws_ref[...])
    merged = merged + gates[:, d:2 * d] * _dot(yd_ref[...], wd_ref[...])
    merged = merged + gates[:, 2 * d:3 * d] * _dot(ym_ref[...], wm_ref[...])
    o_ref[...] = x_ref[...] + _dot(merged.astype(BF16), wo_ref[...])


def _combine(y_ssd, y_dil, y_mem, proj, x2d, w_ssd, w_dil, w_mem, w_out, gate_col_block, tm=512):
    t, d = x2d.shape
    const = lambda i: (0, 0)
    row = lambda i: (i, 0)
    return pl.pallas_call(
        _combine_kernel,
        grid=(t // tm,),
        in_specs=[
            pl.BlockSpec((tm, y_ssd.shape[1]), row),
            pl.BlockSpec((tm, y_dil.shape[1]), row),
            pl.BlockSpec((tm, y_mem.shape[1]), row),
            pl.BlockSpec((tm, 3 * d), lambda i: (i, gate_col_block)),
            pl.BlockSpec((tm, d), row),
            pl.BlockSpec(w_ssd.shape, const),
            pl.BlockSpec(w_dil.shape, const),
            pl.BlockSpec(w_mem.shape, const),
            pl.BlockSpec(w_out.shape, const),
        ],
        out_specs=pl.BlockSpec((tm, d), row),
        out_shape=jax.ShapeDtypeStruct((t, d), F32),
        compiler_params=_cparams("parallel"),
        name="combine",
    )(y_ssd, y_dil, y_mem, proj, x2d, w_ssd, w_dil, w_mem, w_out)


RANK_CODE = -(2.0 ** 100)


def _top_rows(work, out_ref, count):
    for r in range(count):
        m = jnp.max(work, axis=0, keepdims=True)
        out_ref[r:r + 1, :] = m
        work = jnp.where(work == m, (r + 1) * RANK_CODE, work)
    return work


def _gelu(x):
    return 0.5 * x * (1.0 + lax.erf(x * (1.0 / math.sqrt(2.0))))


def _peer_kernel(x_ref, gain_ref, wq_ref, k1_ref, k2_ref, u_ref, vt_ref, o_ref,
                 hn_ref, cnt_ref, rank2_ref, p1_ref, p2_ref, top1_ref, top2_ref, cand_ref, sc_ref,
                 g_ref, acc_ref):
    e = pl.program_id(1)
    n_e = pl.num_programs(1)
    tt = x_ref.shape[0]
    eb = u_ref.shape[0]
    k = PEER_TOPK
    nk = PEER_KEYS
    n_tc = tt // LANES
    n_grp = tt // PEER_GROUP_TOKENS
    n_i = eb // nk
    assert n_i % 8 == 0, "whole aligned 8-row groups of key-1 rows per expert block"

    @pl.when(e == 0)
    def _():
        x = x_ref[...]
        ms = jnp.mean(x * x, axis=-1, keepdims=True)
        hn = x * lax.rsqrt(ms + EPS) * gain_ref[...]
        hn_t = hn.T.astype(BF16)
        for grp in range(n_grp):
            hn_ref[grp] = hn_t[:, grp * PEER_GROUP_TOKENS:(grp + 1) * PEER_GROUP_TOKENS]
        acc_ref[...] = jnp.zeros(acc_ref.shape, F32)
        for h in range(PEER_HEADS):
            r0 = h * 2 * PEER_HALF
            q1 = _dot(wq_ref[r0:r0 + PEER_HALF, :], hn_t)
            q2 = _dot(wq_ref[r0 + PEER_HALF:r0 + 2 * PEER_HALF, :], hn_t)
            s1 = _dot_f32(k1_ref[...], q1)
            s2 = _dot_f32(k2_ref[...], q2)
            ranked1 = _top_rows(s1, top1_ref, k)
            ranked2 = _top_rows(s2, top2_ref, k)
            v1 = top1_ref[...]
            v2 = top2_ref[...]
            brow = lax.broadcasted_iota(jnp.int32, (k, 1), 0)
            groups = []
            for a in range(k // 2):
                nb = k // (a + 1)
                rows_b = 8 if nb <= 8 else k
                blk = v1[a:a + 1, :] + v2[0:rows_b, :]
                groups.append(jnp.where(brow[0:rows_b] < nb, blk, -jnp.inf))
            groups.append(v1[k // 2:k, :] + v2[0:1, :])
            cand = jnp.concatenate(groups, axis=0)
            cand_ref[0:cand.shape[0], :] = cand
            _top_rows(cand_ref[0:cand.shape[0], :], sc_ref, k)
            sc = sc_ref[...]
            tau = sc[k - 1:k, :]
            z = jnp.sum(jnp.exp(sc - sc[0:1, :]), axis=0, keepdims=True)
            p1 = jnp.exp(s1 - v1[0:1, :])
            p2 = jnp.exp(s2 - v2[0:1, :]) / z
            cnt = jnp.zeros_like(s1)
            for a in range(k):
                cnt_a = jnp.sum(jnp.where(v1[a:a + 1, :] + v2 >= tau, 1.0, 0.0), axis=0, keepdims=True)
                cnt = jnp.where(ranked1 == (a + 1) * RANK_CODE, cnt_a, cnt)
            rank2 = jnp.where(ranked2 <= RANK_CODE, ranked2 * (1.0 / RANK_CODE) - 1.0, PEER_UNRANKED)
            for tc in range(n_tc):
                ts = slice(tc * LANES, (tc + 1) * LANES)
                cnt_ref[h, tc] = cnt[:, ts]
                p1_ref[h, tc] = p1[:, ts]
                rank2_ref[h, tc] = rank2[:, ts]
                p2_ref[h, tc] = p2[:, ts]

    i0 = pl.multiple_of(e * n_i, n_i)

    def gate_body(tc, carry):
        cnt_rows = [cnt_ref[h, tc, pl.ds(i0, n_i), :] for h in range(PEER_HEADS)]
        p1_rows = [p1_ref[h, tc, pl.ds(i0, n_i), :] for h in range(PEER_HEADS)]
        for ii in range(n_i):
            gsum = jnp.zeros((nk, LANES), F32)
            for h in range(PEER_HEADS):
                chosen = rank2_ref[h, tc] < cnt_rows[h][ii:ii + 1, :]
                gsum = gsum + jnp.where(chosen, p2_ref[h, tc] * p1_rows[h][ii:ii + 1, :], 0.0)
            g_ref[tc, ii * nk:(ii + 1) * nk, :] = gsum
        return carry

    lax.fori_loop(0, n_tc, gate_body, 0)

    hn_all = jnp.concatenate([hn_ref[grp] for grp in range(n_grp)], axis=1)
    act = _gelu(_dot(u_ref[...], hn_all))
    w = jnp.concatenate([(g_ref[tc] * act[:, tc * LANES:(tc + 1) * LANES]).astype(BF16) for tc in range(n_tc)],
                        axis=1)
    down = _dot(vt_ref[...], w)
    for grp in range(n_grp):
        acc_ref[grp] += down[:, grp * PEER_GROUP_TOKENS:(grp + 1) * PEER_GROUP_TOKENS]

    @pl.when(e == n_e - 1)
    def _():
        for grp in range(n_grp):
            cols = slice(grp * PEER_GROUP_TOKENS, (grp + 1) * PEER_GROUP_TOKENS)
            o_ref[cols, :] = x_ref[cols, :] + acc_ref[grp].T


def _peer(x2d, gain, wq_t, keys1, keys2, u_b, v_b, tt=512, eb=2048):
    t, d = x2d.shape
    n_exp = u_b.shape[0]
    vt_b = v_b.reshape(n_exp // eb, eb, d).transpose(0, 2, 1)
    const = lambda i, e: (0, 0)
    n_cand = 8 * 10
    n_tc = tt // LANES
    return pl.pallas_call(
        _peer_kernel,
        grid=(t // tt, n_exp // eb),
        in_specs=[
            pl.BlockSpec((tt, d), lambda i, e: (i, 0)),
            pl.BlockSpec((1, d), const),
            pl.BlockSpec(wq_t.shape, const),
            pl.BlockSpec(keys1.shape, const),
            pl.BlockSpec(keys2.shape, const),
            pl.BlockSpec((eb, d), lambda i, e: (e, 0)),
            pl.BlockSpec((None, d, eb), lambda i, e: (e, 0, 0)),
        ],
        out_specs=pl.BlockSpec((tt, d), lambda i, e: (i, 0)),
        out_shape=jax.ShapeDtypeStruct((t, d), F32),
        scratch_shapes=[
            pltpu.VMEM((tt // PEER_GROUP_TOKENS, d, PEER_GROUP_TOKENS), BF16),
            pltpu.VMEM((PEER_HEADS, n_tc, PEER_KEYS, LANES), F32),
            pltpu.VMEM((PEER_HEADS, n_tc, PEER_KEYS, LANES), F32),
            pltpu.VMEM((PEER_HEADS, n_tc, PEER_KEYS, LANES), F32),
            pltpu.VMEM((PEER_HEADS, n_tc, PEER_KEYS, LANES), F32),
            pltpu.VMEM((PEER_TOPK, tt), F32),
            pltpu.VMEM((PEER_TOPK, tt), F32),
            pltpu.VMEM((n_cand, tt), F32),
            pltpu.VMEM((PEER_TOPK, tt), F32),
            pltpu.VMEM((n_tc, eb, LANES), F32),
            pltpu.VMEM((tt // PEER_GROUP_TOKENS, d, PEER_GROUP_TOKENS), F32),
        ],
        compiler_params=_cparams("parallel", "arbitrary"),
        name="peer",
    )(x2d, gain, wq_t, keys1, keys2, u_b, vt_b)


def _pad_lanes(v, fill=0.0):
    return jnp.pad(v, ((0, 0), (0, LANES - v.shape[1])), constant_values=fill)


def _layer(x2d, mem2d, pos2d, bsz, seq, mem_len, norm_mix, w_in, conv_w, conv_b, dt_bias, a_log, d_skip,
           ssd_norm, dil_q_norm, dil_k_norm, mem_norm, w_mem_kv, mem_q_norm, mem_k_norm,
           w_up_ssd, w_up_dil, w_up_mem, w_out, norm_ffn, peer_w_q, peer_keys1, peer_keys2, peer_u, peer_v):
    d_inner = SSD_GROUPS * SSD_GROUP_WIDTH
    conv_dim = d_inner + 2 * SSD_GROUPS * SSD_STATE
    n_heads = SSD_GROUPS * SSD_HEADS_PER_GROUP
    dil_w = 3 * DIL_GROUP_WIDTH
    mem_w = MEM_HEADS * MEM_HEAD_DIM
    d = x2d.shape[1]

    o = 0
    parts = {}
    for name, width in (("z", d_inner), ("xbc", conv_dim), ("dt", n_heads), ("q_d", dil_w), ("k_d", dil_w),
                        ("v_d", dil_w), ("q_m", mem_w), ("gates", 3 * d)):
        parts[name] = w_in[:, o:o + width]
        o += width
    w_main = jnp.concatenate([parts[k] for k in ("xbc", "z", "q_d", "k_d", "v_d", "q_m", "gates")],
                             axis=1).astype(BF16)
    w_dt = _pad_lanes(parts["dt"])
    z_off = conv_dim
    qd_off = z_off + d_inner
    qm_off = qd_off + 3 * dil_w
    gate_off = qm_off + mem_w

    proj, dt_raw = _in_proj(x2d, norm_mix[None, :], w_main, w_dt)

    heads = jnp.arange(LANES)[:, None]
    chans = jnp.arange(d_inner)[None, :]
    expand = (chans // SSD_HEAD_DIM == heads).astype(BF16)
    y_ssd = _ssd(proj, dt_raw, conv_w, conv_b[None, :], _pad_lanes(dt_bias[None, :]), _pad_lanes(a_log[None, :]),
                 jnp.repeat(d_skip, SSD_HEAD_DIM)[None, :], ssd_norm[None, :], expand, bsz, seq)

    half = DIL_HEAD_DIM // 2
    inv_freq = ROPE_THETA ** (-jnp.arange(half, dtype=F32) / half)
    inv_freq = jnp.tile(inv_freq, LANES // half)[None, :]
    lane = jnp.arange(LANES)
    blockdiag = (lane[:, None] // DIL_HEAD_DIM == lane[None, :] // DIL_HEAD_DIM).astype(BF16)
    qkv = _dil_prep(proj, pos2d, jnp.tile(dil_q_norm, LANES // DIL_HEAD_DIM)[None, :],
                    jnp.tile(dil_k_norm, LANES // DIL_HEAD_DIM)[None, :], inv_freq, blockdiag,
                    qd_off // DIL_GROUP_WIDTH)
    y_dil = _dil_attn(qkv, bsz, seq)

    k_mem, v_mem = _mem_kv(mem2d, mem_norm[None, :], w_mem_kv.astype(BF16), mem_k_norm[None, :], bsz, mem_len)
    y_mem = _mem_attn(proj, k_mem, v_mem, mem_q_norm[None, :], bsz, seq, mem_len, qm_off // mem_w)

    x_mid = _combine(y_ssd, y_dil, y_mem, proj, x2d, w_up_ssd.astype(BF16), w_up_dil.astype(BF16),
                     w_up_mem.astype(BF16), w_out.astype(BF16), gate_off // (3 * d))

    return _peer(x_mid, norm_ffn[None, :], peer_w_q.T.astype(BF16), peer_keys1, peer_keys2,
                 peer_u.astype(BF16), peer_v.astype(BF16))


def kernel(x, mem, positions, norm_mix, w_in, conv_w, conv_b, dt_bias, a_log, d_skip, ssd_norm, dil_q_norm,
           dil_k_norm, mem_norm, w_mem_kv, mem_q_norm, mem_k_norm, w_up_ssd, w_up_dil, w_up_mem, w_out,
           norm_ffn, peer_w_q, peer_keys1, peer_keys2, peer_u, peer_v):
    bsz, seq, d = x.shape
    mem_len = mem.shape[1]
    x2d = x.reshape(bsz * seq, d)
    mem2d = mem.reshape(bsz * mem_len, d)
    pos2d = positions.reshape(bsz * seq, 1)
    for layer in range(norm_mix.shape[0]):
        x2d = _layer(x2d, mem2d, pos2d, bsz, seq, mem_len, norm_mix[layer], w_in[layer], conv_w[layer],
                     conv_b[layer], dt_bias[layer], a_log[layer], d_skip[layer], ssd_norm[layer],
                     dil_q_norm[layer], dil_k_norm[layer], mem_norm[layer], w_mem_kv[layer], mem_q_norm[layer],
                     mem_k_norm[layer], w_up_ssd[layer], w_up_dil[layer], w_up_mem[layer], w_out[layer],
                     norm_ffn[layer], peer_w_q[layer], peer_keys1[layer], peer_keys2[layer], peer_u[layer],
                     peer_v[layer])
    return x2d.reshape(bsz, seq, d)
```

```python
import functools
import math

import jax
import jax.numpy as jnp
from jax import lax
from jax.experimental import pallas as pl
from jax.experimental.pallas import tpu as pltpu

F32 = jnp.float32
BF16 = jnp.bfloat16
EPS = 1e-6
NEG = -1e30

VMEM_LIMIT_BYTES = 56 * 1024 * 1024
LANES = 128

SSD_CHUNK = 128
SSD_HEAD_DIM = 64
SSD_STATE = 128
SSD_GROUPS = 8
SSD_HEADS_PER_GROUP = 4
SSD_GROUP_WIDTH = SSD_HEADS_PER_GROUP * SSD_HEAD_DIM
SSD_CONV = 4
SSD_HISTORY_ROWS = 16
SSD_CONV_COLS = 512

DIL_DILATIONS = (1, 4, 16)
DIL_BLOCK = 128
DIL_SUPER = DIL_BLOCK * 16
DIL_HEAD_DIM = 64
DIL_GROUP_WIDTH = 8 * DIL_HEAD_DIM
ROPE_THETA = 10000.0

MEM_HEADS = 4
MEM_HEAD_DIM = 384

PEER_KEYS = 128
PEER_HEADS = 8
PEER_TOPK = 16
PEER_HALF = 128
PEER_UNRANKED = 255.0
PEER_GROUP_TOKENS = 256


def _cparams(*sem):
    return pltpu.CompilerParams(dimension_semantics=sem, vmem_limit_bytes=VMEM_LIMIT_BYTES)


def _dot(a, b):
    return jnp.dot(a, b, preferred_element_type=F32)


def _dot_nt(a, b):
    return lax.dot_general(a, b, (((1,), (1,)), ((), ())), preferred_element_type=F32)


def _split2(x):
    hi = x.astype(BF16)
    lo = (x - hi.astype(F32)).astype(BF16)
    return hi, lo


def _split3(x):
    hi = x.astype(BF16)
    r = x - hi.astype(F32)
    mid = r.astype(BF16)
    lo = (r - mid.astype(F32)).astype(BF16)
    return hi, mid, lo


def _dot_f32(a, b):
    ah, al = _split2(a)
    bh, bl = _split2(b)
    return _dot(ah, bh) + (_dot(ah, bl) + _dot(al, bh))


def _dot_sel_right(m, e):
    hi, mid, lo = _split3(m)
    return _dot(hi, e) + (_dot(mid, e) + _dot(lo, e))


def _dot_sel_left(e, m):
    hi, mid, lo = _split3(m)
    return _dot(e, hi) + (_dot(e, mid) + _dot(e, lo))


def _sigmoid(x):
    return 1.0 / (1.0 + jnp.exp(-x))


def _silu(x):
    return x * _sigmoid(x)


def _in_proj_kernel(x_ref, g_ref, w_ref, wdt_ref, o_ref, dt_ref, h_ref):
    @pl.when(pl.program_id(1) == 0)
    def _():
        x = x_ref[...]
        ms = jnp.mean(x * x, axis=-1, keepdims=True)
        h = x * lax.rsqrt(ms + EPS) * g_ref[...]
        h_ref[...] = h.astype(BF16)
        dt_ref[...] = _dot_f32(h, wdt_ref[...])

    o_ref[...] = _dot(h_ref[...], w_ref[...]).astype(o_ref.dtype)


def _in_proj(x2d, gain, w_main, w_dt, tm=1024, tn=1536):
    t, d = x2d.shape
    n = w_main.shape[1]
    w_tiles = w_main.reshape(d, n // tn, tn).transpose(1, 0, 2)
    return pl.pallas_call(
        _in_proj_kernel,
        grid=(t // tm, n // tn),
        in_specs=[
            pl.BlockSpec((tm, d), lambda i, j: (i, 0)),
            pl.BlockSpec((1, d), lambda i, j: (0, 0)),
            pl.BlockSpec((None, d, tn), lambda i, j: (j, 0, 0)),
            pl.BlockSpec((d, LANES), lambda i, j: (0, 0)),
        ],
        out_specs=[
            pl.BlockSpec((tm, tn), lambda i, j: (i, j)),
            pl.BlockSpec((tm, LANES), lambda i, j: (i, 0)),
        ],
        out_shape=[jax.ShapeDtypeStruct((t, n), BF16), jax.ShapeDtypeStruct((t, LANES), F32)],
        scratch_shapes=[pltpu.VMEM((tm, d), BF16)],
        compiler_params=_cparams("parallel", "arbitrary"),
        name="in_proj",
    )(x2d, gain, w_tiles, w_dt)


def _ssd_kernel(xbc_ref, z_ref, dt_ref, cw_ref, cb_ref, dtb_ref, alog_ref, dskip_ref, gain_ref, e_ref, shift_ref,
                o_ref, xh_ref, u_ref, state_ref, y_ref):
    q = SSD_CHUNK
    d_inner = SSD_GROUPS * SSD_GROUP_WIDTH
    gn = SSD_GROUPS * SSD_STATE

    hist = SSD_HISTORY_ROWS

    @pl.when(pl.program_id(1) == 0)
    def _():
        xh_ref[0:hist, :] = jnp.zeros((hist, xh_ref.shape[1]), BF16)
        state_ref[...] = jnp.zeros(state_ref.shape, F32)

    xh_ref[hist:hist + q, :] = xbc_ref[...]
    for cblk in range(xh_ref.shape[1] // SSD_CONV_COLS):
        cs = slice(cblk * SSD_CONV_COLS, (cblk + 1) * SSD_CONV_COLS)
        delayed = _dot(shift_ref[...], xh_ref[:, cs])
        acc = cb_ref[:, cs] + cw_ref[3:4, cs] * xbc_ref[:, cs].astype(F32)
        for k in range(1, SSD_CONV):
            acc = acc + cw_ref[SSD_CONV - 1 - k:SSD_CONV - k, cs] * delayed[(k - 1) * q:k * q, :]
        u_ref[:, cs] = _silu(acc)
    xh_ref[0:hist, :] = xh_ref[q:q + hist, :]

    pre = dt_ref[...] + dtb_ref[...]
    dt = jnp.maximum(pre, 0.0) + jnp.log1p(jnp.exp(-jnp.abs(pre)))
    adt = dt * (-jnp.exp(alog_ref[...]))
    row = lax.broadcasted_iota(jnp.int32, (q, q), 0)
    col = lax.broadcasted_iota(jnp.int32, (q, q), 1)
    causal = row >= col
    tril = jnp.where(causal, 1.0, 0.0).astype(BF16)
    a_cs = _dot_sel_left(tril, adt)
    a_cs_t = a_cs.T
    dt_x = _dot(dt.astype(BF16), e_ref[...])
    acs_x = _dot_sel_right(a_cs, e_ref[...])
    ecs_x = jnp.exp(acs_x)
    dstate_x = jnp.exp(acs_x[q - 1:q, :] - acs_x)
    chunk_decay_x = ecs_x[q - 1:q, :]

    cbs, xds = [], []
    for g in range(SSD_GROUPS):
        c0 = g * SSD_GROUP_WIDTH
        xs_g = u_ref[:, c0:c0 + SSD_GROUP_WIDTH]
        b_g = u_ref[:, d_inner + g * SSD_STATE:d_inner + (g + 1) * SSD_STATE]
        c_g = u_ref[:, d_inner + gn + g * SSD_STATE:d_inner + gn + (g + 1) * SSD_STATE].astype(BF16)
        b_gt = b_g.T.astype(BF16)
        cbs.append(_dot(c_g, b_gt))
        xd = xs_g * dt_x[:, c0:c0 + SSD_GROUP_WIDTH]
        xds.append(xd.astype(BF16))
        xs_b = (xd * dstate_x[:, c0:c0 + SSD_GROUP_WIDTH]).astype(BF16)
        new_state = _dot(b_gt, xs_b)
        s_in = state_ref[g]
        y_off = _dot(c_g, s_in.astype(BF16)) * ecs_x[:, c0:c0 + SSD_GROUP_WIDTH]
        state_ref[g] = s_in * chunk_decay_x[:, c0:c0 + SSD_GROUP_WIDTH] + new_state
        y_ref[:, c0:c0 + SSD_GROUP_WIDTH] = y_off + dskip_ref[:, c0:c0 + SSD_GROUP_WIDTH] * xs_g

    masked = []
    for hd in range(SSD_GROUPS * SSD_HEADS_PER_GROUP):
        seg = a_cs[:, hd:hd + 1] - a_cs_t[hd:hd + 1, :]
        decay = jnp.where(causal, jnp.exp(jnp.where(causal, seg, 0.0)), 0.0)
        masked.append((cbs[hd // SSD_HEADS_PER_GROUP] * decay).astype(BF16))

    lane = lax.broadcasted_iota(jnp.int32, (1, SSD_GROUP_WIDTH), 1)
    head_ind = [jnp.where(lane // SSD_HEAD_DIM == r, 1.0, 0.0).astype(BF16) for r in range(SSD_HEADS_PER_GROUP)]
    for g in range(SSD_GROUPS):
        c0 = g * SSD_GROUP_WIDTH
        y_diag = None
        for r in range(SSD_HEADS_PER_GROUP):
            part = _dot(masked[g * SSD_HEADS_PER_GROUP + r], xds[g] * head_ind[r])
            y_diag = part if y_diag is None else y_diag + part
        y_ref[:, c0:c0 + SSD_GROUP_WIDTH] += y_diag

    yz = y_ref[...] * _silu(z_ref[...].astype(F32))
    ms = jnp.mean(yz * yz, axis=-1, keepdims=True)
    o_ref[...] = (yz * lax.rsqrt(ms + EPS) * gain_ref[...]).astype(o_ref.dtype)


def _ssd(proj, dt_raw, conv_w, conv_b, dt_bias, a_log, d_skip_x, gain, expand, bsz, seq):
    q = SSD_CHUNK
    nc = seq // q
    d_inner = SSD_GROUPS * SSD_GROUP_WIDTH
    conv_dim = conv_w.shape[1]
    const = lambda b, c: (0, 0)
    hist = SSD_HISTORY_ROWS
    out_row = jnp.arange((SSD_CONV - 1) * q)[:, None]
    src_col = jnp.arange(hist + q)[None, :]
    shift = (src_col == hist + out_row % q - (out_row // q + 1)).astype(BF16)
    return pl.pallas_call(
        _ssd_kernel,
        grid=(bsz, nc),
        in_specs=[
            pl.BlockSpec((q, conv_dim), lambda b, c: (b * nc + c, 0)),
            pl.BlockSpec((q, d_inner), lambda b, c: (b * nc + c, conv_dim // d_inner)),
            pl.BlockSpec((q, LANES), lambda b, c: (b * nc + c, 0)),
            pl.BlockSpec((SSD_CONV, conv_dim), const),
            pl.BlockSpec((1, conv_dim), const),
            pl.BlockSpec((1, LANES), const),
            pl.BlockSpec((1, LANES), const),
            pl.BlockSpec((1, d_inner), const),
            pl.BlockSpec((1, d_inner), const),
            pl.BlockSpec((LANES, d_inner), const),
            pl.BlockSpec(((SSD_CONV - 1) * q, hist + q), const),
        ],
        out_specs=pl.BlockSpec((q, d_inner), lambda b, c: (b * nc + c, 0)),
        out_shape=jax.ShapeDtypeStruct((bsz * seq, d_inner), BF16),
        scratch_shapes=[
            pltpu.VMEM((hist + q, conv_dim), BF16),
            pltpu.VMEM((q, conv_dim), F32),
            pltpu.VMEM((SSD_GROUPS, SSD_STATE, SSD_GROUP_WIDTH), F32),
            pltpu.VMEM((q, d_inner), F32),
        ],
        compiler_params=_cparams("parallel", "arbitrary"),
        name="ssd",
    )(proj, proj, dt_raw, conv_w, conv_b, dt_bias, a_log, d_skip_x, gain, expand, shift)


def _dil_prep_kernel(in_ref, pos_ref, qg_ref, kg_ref, invf_ref, bd_ref, o_ref, cos_ref, sin_ref, tmp_ref):
    kind = pl.program_id(1)
    g = pl.program_id(2)
    rows = in_ref.shape[0]

    @pl.when(jnp.logical_and(kind == 0, g == 0))
    def _():
        ang = pos_ref[...].astype(F32) * invf_ref[...]
        lane = lax.broadcasted_iota(jnp.int32, (1, LANES), 1)
        first_half = (lane % DIL_HEAD_DIM) < (DIL_HEAD_DIM // 2)
        cos_ref[...] = jnp.cos(ang)
        sin_ref[...] = jnp.sin(ang) * jnp.where(first_half, -1.0, 1.0)

    @pl.when(kind < 2)
    def _():
        gain = jnp.where(kind == 0, qg_ref[...] * (1.0 / math.sqrt(DIL_HEAD_DIM)), kg_ref[...])
        lane = lax.broadcasted_iota(jnp.int32, (1, LANES), 1)
        first_half = (lane % DIL_HEAD_DIM) < (DIL_HEAD_DIM // 2)
        for c in range(DIL_GROUP_WIDTH // LANES):
            xc = in_ref[:, c * LANES:(c + 1) * LANES].astype(F32)
            ss = _dot((xc * xc).astype(BF16), bd_ref[...])
            xn = xc * lax.rsqrt(ss * (1.0 / DIL_HEAD_DIM) + EPS) * gain
            partner = jnp.where(first_half, pltpu.roll(xn, LANES - DIL_HEAD_DIM // 2, 1),
                                pltpu.roll(xn, DIL_HEAD_DIM // 2, 1))
            tmp_ref[c] = xn * cos_ref[...] + partner * sin_ref[...]

    @pl.when(kind == 2)
    def _():
        for c in range(DIL_GROUP_WIDTH // LANES):
            tmp_ref[c] = in_ref[:, c * LANES:(c + 1) * LANES].astype(F32)

    @pl.when(g == 0)
    def _():
        for c in range(DIL_GROUP_WIDTH // LANES):
            o_ref[:, c * LANES:(c + 1) * LANES] = tmp_ref[c].astype(o_ref.dtype)

    for gi in (1, 2):
        d = DIL_DILATIONS[gi]

        @pl.when(g == gi)
        def _(d=d):
            for unit in range(rows // DIL_BLOCK):
                s, rho = unit // d, unit % d
                for c in range(DIL_GROUP_WIDTH // LANES):
                    src = tmp_ref[c, pl.ds(s * DIL_BLOCK * d + rho, DIL_BLOCK, stride=d), :]
                    o_ref[unit * DIL_BLOCK:(unit + 1) * DIL_BLOCK, c * LANES:(c + 1) * LANES] = src.astype(o_ref.dtype)


def _dil_prep(proj, pos2d, q_gain, k_gain, inv_freq, blockdiag, first_col_block):
    t = proj.shape[0]
    rows = DIL_SUPER
    w = DIL_GROUP_WIDTH
    const = lambda i, k, g: (0, 0)
    return pl.pallas_call(
        _dil_prep_kernel,
        grid=(t // rows, 3, 3),
        in_specs=[
            pl.BlockSpec((rows, w), lambda i, k, g: (i, first_col_block + k * 3 + g)),
            pl.BlockSpec((rows, 1), lambda i, k, g: (i, 0)),
            pl.BlockSpec((1, LANES), const),
            pl.BlockSpec((1, LANES), const),
            pl.BlockSpec((1, LANES), const),
            pl.BlockSpec((LANES, LANES), const),
        ],
        out_specs=pl.BlockSpec((None, rows, w), lambda i, k, g: (k * 3 + g, i, 0)),
        out_shape=jax.ShapeDtypeStruct((9, t, w), BF16),
        scratch_shapes=[pltpu.VMEM((rows, LANES), F32), pltpu.VMEM((rows, LANES), F32),
                        pltpu.VMEM((w // LANES, rows, LANES), F32)],
        compiler_params=_cparams("parallel", "arbitrary", "arbitrary"),
        name="dil_prep",
    )(proj, pos2d, q_gain, k_gain, inv_freq, blockdiag)


def _dil_attn_kernel(q_ref, kc_ref, kp_ref, vc_ref, vp_ref, o_ref, acc_ref, m_ref, l_ref):
    n = pl.program_id(1)
    g = pl.program_id(2)
    rows = q_ref.shape[0]
    blk = DIL_BLOCK
    n_units = rows // blk
    first_gi = len(DIL_DILATIONS) - 1
    last_gi = 0

    qi = lax.broadcasted_iota(jnp.int32, (blk, blk), 0)
    kj = lax.broadcasted_iota(jnp.int32, (blk, blk), 1)
    mask_cur = kj <= qi
    mask_prev = kj >= qi
    first_head = lax.broadcasted_iota(jnp.int32, (blk, LANES), 1) < DIL_HEAD_DIM
    head_ind = (jnp.where(first_head, 1.0, 0.0).astype(BF16), jnp.where(first_head, 0.0, 1.0).astype(BF16))

    def run_group(gi):
        d = DIL_DILATIONS[gi]

        def unit_body(u, carry, *, from_prev_block):
            q_row = pl.multiple_of(u * blk, blk)
            c_row = q_row
            if from_prev_block:
                kp_src, vp_src = kp_ref, vp_ref
                p_row = pl.multiple_of(rows + u * blk - d * blk, blk)
                prev_bias = jnp.where(n > 0, 0.0, NEG)
            else:
                kp_src, vp_src = kc_ref, vc_ref
                p_row = pl.multiple_of(u * blk - d * blk, blk)
                prev_bias = 0.0
            pos_row = (u // d) * (blk * d) + (u % d)
            pos_rows = pl.ds(pos_row, blk, stride=d) if d > 1 else pl.ds(pl.multiple_of(u * blk, blk), blk)
            n_hp = DIL_GROUP_WIDTH // LANES
            old = [] if gi == first_gi else [(m_ref[hp, pos_rows, :], acc_ref[hp, pos_rows, :], l_ref[hp, pos_rows, :])
                                             for hp in range(n_hp)]
            n_heads = DIL_GROUP_WIDTH // DIL_HEAD_DIM
            head_lanes = [slice(h * DIL_HEAD_DIM, (h + 1) * DIL_HEAD_DIM) for h in range(n_heads)]
            scores = []
            for lanes in head_lanes:
                qh = q_ref[pl.ds(q_row, blk), lanes]
                s_cur = jnp.where(mask_cur, _dot_nt(qh, kc_ref[pl.ds(c_row, blk), lanes]), NEG)
                s_prev = jnp.where(mask_prev, _dot_nt(qh, kp_src[pl.ds(p_row, blk), lanes]), NEG) + prev_bias
                scores.append((s_cur, s_prev))
            probs = []
            for s_cur, s_prev in scores:
                m_u = jnp.max(jnp.maximum(s_cur, s_prev), axis=1, keepdims=True)
                probs.append((jnp.exp(s_cur - m_u).astype(BF16), jnp.exp(s_prev - m_u).astype(BF16), m_u))
            new = []
            for hp in range(n_hp):
                pair = slice(hp * LANES, (hp + 1) * LANES)
                v_cur = vc_ref[pl.ds(c_row, blk), pair]
                v_prev = vp_src[pl.ds(p_row, blk), pair]
                ol = None
                for hh, ind in enumerate(head_ind):
                    p_cur, p_prev, _ = probs[hp * 2 + hh]
                    rhs_cur = jnp.concatenate([v_cur * ind, ind], axis=1)
                    rhs_prev = jnp.concatenate([v_prev * ind, ind], axis=1)
                    part = _dot(p_cur, rhs_cur) + _dot(p_prev, rhs_prev)
                    ol = part if ol is None else ol + part
                m_pair = jnp.where(first_head, probs[hp * 2][2], probs[hp * 2 + 1][2])
                new.append((m_pair, ol[:, :LANES], ol[:, LANES:]))
            for hp in range(n_hp):
                m_u, o_u, l_u = new[hp]
                if gi == first_gi:
                    acc_ref[hp, pos_rows, :] = o_u
                    m_ref[hp, pos_rows, :] = m_u
                    l_ref[hp, pos_rows, :] = l_u
                    continue
                m_old, acc_old, l_old = old[hp]
                m_new = jnp.maximum(m_old, m_u)
                a_old = jnp.exp(m_old - m_new)
                a_new = jnp.exp(m_u - m_new)
                acc_new = acc_old * a_old + o_u * a_new
                l_new = l_old * a_old + l_u * a_new
                if gi == last_gi:
                    o_ref[pos_rows, hp * LANES:(hp + 1) * LANES] = (acc_new / l_new).astype(o_ref.dtype)
                else:
                    acc_ref[hp, pos_rows, :] = acc_new
                    l_ref[hp, pos_rows, :] = l_new
                    m_ref[hp, pos_rows, :] = m_new
            return carry

        lax.fori_loop(0, d, functools.partial(unit_body, from_prev_block=True), 0)
        if d < n_units:
            lax.fori_loop(d, n_units, functools.partial(unit_body, from_prev_block=False), 0)

    for gi in range(len(DIL_DILATIONS)):
        pl.when(g == first_gi - gi)(functools.partial(run_group, gi))


def _dil_attn(qkv, bsz, seq):
    rows = DIL_SUPER
    w = DIL_GROUP_WIDTH
    nsb = seq // rows
    last = len(DIL_DILATIONS) - 1
    cur = lambda off: (lambda b, n, g: (off + last - g, b * nsb + n, 0))
    prev = lambda off: (lambda b, n, g: (off + last - g, b * nsb + jnp.maximum(n - 1, 0), 0))
    return pl.pallas_call(
        _dil_attn_kernel,
        grid=(bsz, nsb, 3),
        in_specs=[
            pl.BlockSpec((None, rows, w), cur(0)),
            pl.BlockSpec((None, rows, w), cur(3)),
            pl.BlockSpec((None, rows, w), prev(3)),
            pl.BlockSpec((None, rows, w), cur(6)),
            pl.BlockSpec((None, rows, w), prev(6)),
        ],
        out_specs=pl.BlockSpec((rows, w), lambda b, n, g: (b * nsb + n, 0)),
        out_shape=jax.ShapeDtypeStruct((bsz * seq, w), BF16),
        scratch_shapes=[pltpu.VMEM((w // LANES, rows, LANES), F32), pltpu.VMEM((w // LANES, rows, LANES), F32),
                        pltpu.VMEM((w // LANES, rows, LANES), F32)],
        compiler_params=_cparams("parallel", "arbitrary", "arbitrary"),
        name="dil_attn",
    )(qkv, qkv, qkv, qkv, qkv)


def _mem_kv_kernel(mem_ref, g_ref, w_ref, kg_ref, k_ref, v_ref):
    x = mem_ref[...]
    ms = jnp.mean(x * x, axis=-1, keepdims=True)
    h = (x * lax.rsqrt(ms + EPS) * g_ref[...]).astype(BF16)
    kv = _dot(h, w_ref[...])
    width = MEM_HEADS * MEM_HEAD_DIM
    for hd in range(MEM_HEADS):
        sl = slice(hd * MEM_HEAD_DIM, (hd + 1) * MEM_HEAD_DIM)
        kh = kv[:, sl]
        ms = jnp.mean(kh * kh, axis=-1, keepdims=True)
        k_ref[:, sl] = (kh * lax.rsqrt(ms + EPS) * kg_ref[...]).astype(k_ref.dtype)
    v_ref[...] = kv[:, width:].astype(v_ref.dtype)


def _mem_kv(mem2d, gain, w_kv, k_gain, bsz, mem_len):
    d = mem2d.shape[1]
    width = MEM_HEADS * MEM_HEAD_DIM
    const = lambda b: (0, 0)
    return pl.pallas_call(
        _mem_kv_kernel,
        grid=(bsz,),
        in_specs=[
            pl.BlockSpec((mem_len, d), lambda b: (b, 0)),
            pl.BlockSpec((1, d), const),
            pl.BlockSpec((d, 2 * width), const),
            pl.BlockSpec((1, MEM_HEAD_DIM), const),
        ],
        out_specs=[pl.BlockSpec((mem_len, width), lambda b: (b, 0)),
                   pl.BlockSpec((mem_len, width), lambda b: (b, 0))],
        out_shape=[jax.ShapeDtypeStruct((bsz * mem_len, width), BF16)] * 2,
        compiler_params=_cparams("parallel"),
        name="mem_kv",
    )(mem2d, gain, w_kv, k_gain)


def _mem_attn_kernel(q_ref, k_ref, v_ref, qg_ref, o_ref):
    for hd in range(MEM_HEADS):
        sl = slice(hd * MEM_HEAD_DIM, (hd + 1) * MEM_HEAD_DIM)
        qh = q_ref[:, sl].astype(F32)
        ms = jnp.mean(qh * qh, axis=-1, keepdims=True)
        qn = (qh * lax.rsqrt(ms + EPS) * qg_ref[...]).astype(BF16)
        s = _dot_nt(qn, k_ref[:, sl]) * (1.0 / math.sqrt(MEM_HEAD_DIM))
        p = jnp.exp(s - jnp.max(s, axis=-1, keepdims=True))
        den = jnp.sum(p, axis=-1, keepdims=True)
        o_ref[:, sl] = (_dot(p.astype(BF16), v_ref[:, sl]) / den).astype(o_ref.dtype)


def _mem_attn(proj, k, v, q_gain, bsz, seq, mem_len, q_col_block, tq=1024):
    width = MEM_HEADS * MEM_HEAD_DIM
    nq = seq // tq
    return pl.pallas_call(
        _mem_attn_kernel,
        grid=(bsz, nq),
        in_specs=[
            pl.BlockSpec((tq, width), lambda b, i: (b * nq + i, q_col_block)),
            pl.BlockSpec((mem_len, width), lambda b, i: (b, 0)),
            pl.BlockSpec((mem_len, width), lambda b, i: (b, 0)),
            pl.BlockSpec((1, MEM_HEAD_DIM), lambda b, i: (0, 0)),
        ],
        out_specs=pl.BlockSpec((tq, width), lambda b, i: (b * nq + i, 0)),
        out_shape=jax.ShapeDtypeStruct((bsz * seq, width), BF16),
        compiler_params=_cparams("parallel", "parallel"),
        name="mem_attn",
    )(proj, k, v, q_gain)


def _combine_kernel(ys_ref, yd_ref, ym_ref, gl_ref, x_ref, ws_ref, wd_ref, wm_ref, wo_ref, o_ref):
    d = x_ref.shape[1]
    gates = _sigmoid(gl_ref[...].astype(F32))
    merged = gates[:, 0:d] * _dot(ys_ref[...], ws_ref[...])
    merged = merged + gates[:, d:2 * d] * _dot(yd_ref[...], wd_ref[...])
    merged = merged + gates[:, 2 * d:3 * d] * _dot(ym_ref[...], wm_ref[...])
    o_ref[...] = x_ref[...] + _dot(merged.astype(BF16), wo_ref[...])


def _combine(y_ssd, y_dil, y_mem, proj, x2d, w_ssd, w_dil, w_mem, w_out, gate_col_block, tm=512):
    t, d = x2d.shape
    const = lambda i: (0, 0)
    row = lambda i: (i, 0)
    return pl.pallas_call(
        _combine_kernel,
        grid=(t // tm,),
        in_specs=[
            pl.BlockSpec((tm, y_ssd.shape[1]), row),
            pl.BlockSpec((tm, y_dil.shape[1]), row),
            pl.BlockSpec((tm, y_mem.shape[1]), row),
            pl.BlockSpec((tm, 3 * d), lambda i: (i, gate_col_block)),
            pl.BlockSpec((tm, d), row),
            pl.BlockSpec(w_ssd.shape, const),
            pl.BlockSpec(w_dil.shape, const),
            pl.BlockSpec(w_mem.shape, const),
            pl.BlockSpec(w_out.shape, const),
        ],
        out_specs=pl.BlockSpec((tm, d), row),
        out_shape=jax.ShapeDtypeStruct((t, d), F32),
        compiler_params=_cparams("parallel"),
        name="combine",
    )(y_ssd, y_dil, y_mem, proj, x2d, w_ssd, w_dil, w_mem, w_out)


RANK_CODE = -(2.0 ** 100)


def _top_rows(work, out_ref, count):
    for r in range(count):
        m = jnp.max(work, axis=0, keepdims=True)
        out_ref[r:r + 1, :] = m
        work = jnp.where(work == m, (r + 1) * RANK_CODE, work)
    return work


def _gelu(x):
    return 0.5 * x * (1.0 + lax.erf(x * (1.0 / math.sqrt(2.0))))


def _peer_kernel(x_ref, gain_ref, wq_ref, k1_ref, k2_ref, u_ref, vt_ref, o_ref,
                 hn_ref, cnt_ref, rank2_ref, p1_ref, p2_ref, top1_ref, top2_ref, cand_ref, sc_ref,
                 g_ref, acc_ref):
    e = pl.program_id(1)
    n_e = pl.num_programs(1)
    tt = x_ref.shape[0]
    eb = u_ref.shape[0]
    k = PEER_TOPK
    nk = PEER_KEYS
    n_tc = tt // LANES
    n_grp = tt // PEER_GROUP_TOKENS
    n_i = eb // nk
    assert n_i % 8 == 0, "whole aligned 8-row groups of key-1 rows per expert block"

    @pl.when(e == 0)
    def _():
        x = x_ref[...]
        ms = jnp.mean(x * x, axis=-1, keepdims=True)
        hn = x * lax.rsqrt(ms + EPS) * gain_ref[...]
        hn_t = hn.T.astype(BF16)
        for grp in range(n_grp):
            hn_ref[grp] = hn_t[:, grp * PEER_GROUP_TOKENS:(grp + 1) * PEER_GROUP_TOKENS]
        acc_ref[...] = jnp.zeros(acc_ref.shape, F32)
        for h in range(PEER_HEADS):
            r0 = h * 2 * PEER_HALF
            q1 = _dot(wq_ref[r0:r0 + PEER_HALF, :], hn_t)
            q2 = _dot(wq_ref[r0 + PEER_HALF:r0 + 2 * PEER_HALF, :], hn_t)
            s1 = _dot_f32(k1_ref[...], q1)
            s2 = _dot_f32(k2_ref[...], q2)
            ranked1 = _top_rows(s1, top1_ref, k)
            ranked2 = _top_rows(s2, top2_ref, k)
            v1 = top1_ref[...]
            v2 = top2_ref[...]
            brow = lax.broadcasted_iota(jnp.int32, (k, 1), 0)
            groups = []
            for a in range(k // 2):
                nb = k // (a + 1)
                rows_b = 8 if nb <= 8 else k
                blk = v1[a:a + 1, :] + v2[0:rows_b, :]
                groups.append(jnp.where(brow[0:rows_b] < nb, blk, -jnp.inf))
            groups.append(v1[k // 2:k, :] + v2[0:1, :])
            cand = jnp.concatenate(groups, axis=0)
            cand_ref[0:cand.shape[0], :] = cand
            _top_rows(cand_ref[0:cand.shape[0], :], sc_ref, k)
            sc = sc_ref[...]
            tau = sc[k - 1:k, :]
            z = jnp.sum(jnp.exp(sc - sc[0:1, :]), axis=0, keepdims=True)
            p1 = jnp.exp(s1 - v1[0:1, :])
            p2 = jnp.exp(s2 - v2[0:1, :]) / z
            cnt = jnp.zeros_like(s1)
            for a in range(k):
                cnt_a = jnp.sum(jnp.where(v1[a:a + 1, :] + v2 >= tau, 1.0, 0.0), axis=0, keepdims=True)
                cnt = jnp.where(ranked1 == (a + 1) * RANK_CODE, cnt_a, cnt)
            rank2 = jnp.where(ranked2 <= RANK_CODE, ranked2 * (1.0 / RANK_CODE) - 1.0, PEER_UNRANKED)
            for tc in range(n_tc):
                ts = slice(tc * LANES, (tc + 1) * LANES)
                cnt_ref[h, tc] = cnt[:, ts]
                p1_ref[h, tc] = p1[:, ts]
                rank2_ref[h, tc] = rank2[:, ts]
                p2_ref[h, tc] = p2[:, ts]

    i0 = pl.multiple_of(e * n_i, n_i)

    def gate_body(tc, carry):
        cnt_rows = [cnt_ref[h, tc, pl.ds(i0, n_i), :] for h in range(PEER_HEADS)]
        p1_rows = [p1_ref[h, tc, pl.ds(i0, n_i), :] for h in range(PEER_HEADS)]
        for ii in range(n_i):
            gsum = jnp.zeros((nk, LANES), F32)
            for h in range(PEER_HEADS):
                chosen = rank2_ref[h, tc] < cnt_rows[h][ii:ii + 1, :]
                gsum = gsum + jnp.where(chosen, p2_ref[h, tc] * p1_rows[h][ii:ii + 1, :], 0.0)
            g_ref[tc, ii * nk:(ii + 1) * nk, :] = gsum
        return carry

    lax.fori_loop(0, n_tc, gate_body, 0)

    hn_all = jnp.concatenate([hn_ref[grp] for grp in range(n_grp)], axis=1)
    act = _gelu(_dot(u_ref[...], hn_all))
    w = jnp.concatenate([(g_ref[tc] * act[:, tc * LANES:(tc + 1) * LANES]).astype(BF16) for tc in range(n_tc)],
                        axis=1)
    down = _dot(vt_ref[...], w)
    for grp in range(n_grp):
        acc_ref[grp] += down[:, grp * PEER_GROUP_TOKENS:(grp + 1) * PEER_GROUP_TOKENS]

    @pl.when(e == n_e - 1)
    def _():
        for grp in range(n_grp):
            cols = slice(grp * PEER_GROUP_TOKENS, (grp + 1) * PEER_GROUP_TOKENS)
            o_ref[cols, :] = x_ref[cols, :] + acc_ref[grp].T


def _transpose_cast_kernel(x_ref, o_ref):
    o_ref[...] = x_ref[...].T.astype(o_ref.dtype)


def _blocked_transpose(v, eb, tb=512):
    n_exp, d = v.shape
    per = eb // tb
    return pl.pallas_call(
        _transpose_cast_kernel,
        grid=(n_exp // tb,),
        in_specs=[pl.BlockSpec((tb, d), lambda r: (r, 0))],
        out_specs=pl.BlockSpec((None, d, tb), lambda r: (r // per, 0, r % per)),
        out_shape=jax.ShapeDtypeStruct((n_exp // eb, d, eb), BF16),
        compiler_params=_cparams("parallel"),
        name="peer_v_layout",
    )(v)


def _peer(x2d, gain, wq_t, keys1, keys2, u_b, v, tt=512, eb=2048):
    t, d = x2d.shape
    n_exp = u_b.shape[0]
    vt_b = _blocked_transpose(v, eb)
    const = lambda i, e: (0, 0)
    n_cand = 8 * 10
    n_tc = tt // LANES
    return pl.pallas_call(
        _peer_kernel,
        grid=(t // tt, n_exp // eb),
        in_specs=[
            pl.BlockSpec((tt, d), lambda i, e: (i, 0)),
            pl.BlockSpec((1, d), const),
            pl.BlockSpec(wq_t.shape, const),
            pl.BlockSpec(keys1.shape, const),
            pl.BlockSpec(keys2.shape, const),
            pl.BlockSpec((eb, d), lambda i, e: (e, 0)),
            pl.BlockSpec((None, d, eb), lambda i, e: (e, 0, 0)),
        ],
        out_specs=pl.BlockSpec((tt, d), lambda i, e: (i, 0)),
        out_shape=jax.ShapeDtypeStruct((t, d), F32),
        scratch_shapes=[
            pltpu.VMEM((tt // PEER_GROUP_TOKENS, d, PEER_GROUP_TOKENS), BF16),
            pltpu.VMEM((PEER_HEADS, n_tc, PEER_KEYS, LANES), F32),
            pltpu.VMEM((PEER_HEADS, n_tc, PEER_KEYS, LANES), F32),
            pltpu.VMEM((PEER_HEADS, n_tc, PEER_KEYS, LANES), F32),
            pltpu.VMEM((PEER_HEADS, n_tc, PEER_KEYS, LANES), F32),
            pltpu.VMEM((PEER_TOPK, tt), F32),
            pltpu.VMEM((PEER_TOPK, tt), F32),
            pltpu.VMEM((n_cand, tt), F32),
            pltpu.VMEM((PEER_TOPK, tt), F32),
            pltpu.VMEM((n_tc, eb, LANES), F32),
            pltpu.VMEM((tt // PEER_GROUP_TOKENS, d, PEER_GROUP_TOKENS), F32),
        ],
        compiler_params=_cparams("parallel", "arbitrary"),
        name="peer",
    )(x2d, gain, wq_t, keys1, keys2, u_b, vt_b)


def _pad_lanes(v, fill=0.0):
    return jnp.pad(v, ((0, 0), (0, LANES - v.shape[1])), constant_values=fill)


def _layer(x2d, mem2d, pos2d, bsz, seq, mem_len, norm_mix, w_in, conv_w, conv_b, dt_bias, a_log, d_skip,
           ssd_norm, dil_q_norm, dil_k_norm, mem_norm, w_mem_kv, mem_q_norm, mem_k_norm,
           w_up_ssd, w_up_dil, w_up_mem, w_out, norm_ffn, peer_w_q, peer_keys1, peer_keys2, peer_u, peer_v):
    d_inner = SSD_GROUPS * SSD_GROUP_WIDTH
    conv_dim = d_inner + 2 * SSD_GROUPS * SSD_STATE
    n_heads = SSD_GROUPS * SSD_HEADS_PER_GROUP
    dil_w = 3 * DIL_GROUP_WIDTH
    mem_w = MEM_HEADS * MEM_HEAD_DIM
    d = x2d.shape[1]

    o = 0
    parts = {}
    for name, width in (("z", d_inner), ("xbc", conv_dim), ("dt", n_heads), ("q_d", dil_w), ("k_d", dil_w),
                        ("v_d", dil_w), ("q_m", mem_w), ("gates", 3 * d)):
        parts[name] = w_in[:, o:o + width]
        o += width
    w_main = jnp.concatenate([parts[k].astype(BF16) for k in ("xbc", "z", "q_d", "k_d", "v_d", "q_m", "gates")],
                             axis=1)
    w_dt = _pad_lanes(parts["dt"])
    z_off = conv_dim
    qd_off = z_off + d_inner
    qm_off = qd_off + 3 * dil_w
    gate_off = qm_off + mem_w

    proj, dt_raw = _in_proj(x2d, norm_mix[None, :], w_main, w_dt)

    heads = jnp.arange(LANES)[:, None]
    chans = jnp.arange(d_inner)[None, :]
    expand = (chans // SSD_HEAD_DIM == heads).astype(BF16)
    y_ssd = _ssd(proj, dt_raw, conv_w, conv_b[None, :], _pad_lanes(dt_bias[None, :]), _pad_lanes(a_log[None, :]),
                 jnp.repeat(d_skip, SSD_HEAD_DIM)[None, :], ssd_norm[None, :], expand, bsz, seq)

    half = DIL_HEAD_DIM // 2
    inv_freq = ROPE_THETA ** (-jnp.arange(half, dtype=F32) / half)
    inv_freq = jnp.tile(inv_freq, LANES // half)[None, :]
    lane = jnp.arange(LANES)
    blockdiag = (lane[:, None] // DIL_HEAD_DIM == lane[None, :] // DIL_HEAD_DIM).astype(BF16)
    qkv = _dil_prep(proj, pos2d, jnp.tile(dil_q_norm, LANES // DIL_HEAD_DIM)[None, :],
                    jnp.tile(dil_k_norm, LANES // DIL_HEAD_DIM)[None, :], inv_freq, blockdiag,
                    qd_off // DIL_GROUP_WIDTH)
    y_dil = _dil_attn(qkv, bsz, seq)

    k_mem, v_mem = _mem_kv(mem2d, mem_norm[None, :], w_mem_kv.astype(BF16), mem_k_norm[None, :], bsz, mem_len)
    y_mem = _mem_attn(proj, k_mem, v_mem, mem_q_norm[None, :], bsz, seq, mem_len, qm_off // mem_w)

    x_mid = _combine(y_ssd, y_dil, y_mem, proj, x2d, w_up_ssd.astype(BF16), w_up_dil.astype(BF16),
                     w_up_mem.astype(BF16), w_out.astype(BF16), gate_off // (3 * d))

    return _peer(x_mid, norm_ffn[None, :], peer_w_q.T.astype(BF16), peer_keys1, peer_keys2,
                 peer_u.astype(BF16), peer_v)


def kernel(x, mem, positions, norm_mix, w_in, conv_w, conv_b, dt_bias, a_log, d_skip, ssd_norm, dil_q_norm,
           dil_k_norm, mem_norm, w_mem_kv, mem_q_norm, mem_k_norm, w_up_ssd, w_up_dil, w_up_mem, w_out,
           norm_ffn, peer_w_q, peer_keys1, peer_keys2, peer_u, peer_v):
    bsz, seq, d = x.shape
    mem_len = mem.shape[1]
    x2d = x.reshape(bsz * seq, d)
    mem2d = mem.reshape(bsz * mem_len, d)
    pos2d = positions.reshape(bsz * seq, 1)
    for layer in range(norm_mix.shape[0]):
        x2d = _layer(x2d, mem2d, pos2d, bsz, seq, mem_len, norm_mix[layer], w_in[layer], conv_w[layer],
                     conv_b[layer], dt_bias[layer], a_log[layer], d_skip[layer], ssd_norm[layer],
                     dil_q_norm[layer], dil_k_norm[layer], mem_norm[layer], w_mem_kv[layer], mem_q_norm[layer],
                     mem_k_norm[layer], w_up_ssd[layer], w_up_dil[layer], w_up_mem[layer], w_out[layer],
                     norm_ffn[layer], peer_w_q[layer], peer_keys1[layer], peer_keys2[layer], peer_u[layer],
                     peer_v[layer])
    return x2d.reshape(bsz, seq, d)
```

```python
import functools
import math

import jax
import jax.numpy as jnp
from jax import lax
from jax.experimental import pallas as pl
from jax.experimental.pallas import tpu as pltpu

F32 = jnp.float32
BF16 = jnp.bfloat16
EPS = 1e-6
NEG = -1e30

VMEM_LIMIT_BYTES = 56 * 1024 * 1024
LANES = 128

SSD_CHUNK = 128
SSD_HEAD_DIM = 64
SSD_STATE = 128
SSD_GROUPS = 8
SSD_HEADS_PER_GROUP = 4
SSD_GROUP_WIDTH = SSD_HEADS_PER_GROUP * SSD_HEAD_DIM
SSD_CONV = 4
SSD_HISTORY_ROWS = 16
SSD_CONV_COLS = 512
SSD_CHUNKS_PER_STEP = 4

DIL_DILATIONS = (1, 4, 16)
DIL_BLOCK = 128
DIL_SUPER = DIL_BLOCK * 16
DIL_HEAD_DIM = 64
DIL_GROUP_WIDTH = 8 * DIL_HEAD_DIM
ROPE_THETA = 10000.0

MEM_HEADS = 4
MEM_HEAD_DIM = 384

PEER_KEYS = 128
PEER_HEADS = 8
PEER_TOPK = 16
PEER_HALF = 128
PEER_UNRANKED = 255.0
PEER_GROUP_TOKENS = 256


def _cparams(*sem):
    return pltpu.CompilerParams(dimension_semantics=sem, vmem_limit_bytes=VMEM_LIMIT_BYTES)


def _dot(a, b):
    return jnp.dot(a, b, preferred_element_type=F32)


def _dot_nt(a, b):
    return lax.dot_general(a, b, (((1,), (1,)), ((), ())), preferred_element_type=F32)


def _split2(x):
    hi = x.astype(BF16)
    lo = (x - hi.astype(F32)).astype(BF16)
    return hi, lo


def _split3(x):
    hi = x.astype(BF16)
    r = x - hi.astype(F32)
    mid = r.astype(BF16)
    lo = (r - mid.astype(F32)).astype(BF16)
    return hi, mid, lo


def _dot_f32(a, b):
    ah, al = _split2(a)
    bh, bl = _split2(b)
    return _dot(ah, bh) + (_dot(ah, bl) + _dot(al, bh))


def _dot_sel_right(m, e):
    hi, mid, lo = _split3(m)
    return _dot(hi, e) + (_dot(mid, e) + _dot(lo, e))


def _dot_sel_left(e, m):
    hi, mid, lo = _split3(m)
    return _dot(e, hi) + (_dot(e, mid) + _dot(e, lo))


def _sigmoid(x):
    return 1.0 / (1.0 + jnp.exp(-x))


def _silu(x):
    return x * _sigmoid(x)


def _in_proj_kernel(x_ref, g_ref, w_ref, wdt_ref, o_ref, dt_ref, h_ref):
    @pl.when(pl.program_id(1) == 0)
    def _():
        x = x_ref[...]
        ms = jnp.mean(x * x, axis=-1, keepdims=True)
        h = x * lax.rsqrt(ms + EPS) * g_ref[...]
        h_ref[...] = h.astype(BF16)
        dt_ref[...] = _dot_f32(h, wdt_ref[...])

    o_ref[...] = _dot(h_ref[...], w_ref[...]).astype(o_ref.dtype)


def _in_proj(x2d, gain, w_main, w_dt, tm=1024, tn=1536):
    t, d = x2d.shape
    n = w_main.shape[1]
    w_tiles = w_main.reshape(d, n // tn, tn).transpose(1, 0, 2)
    return pl.pallas_call(
        _in_proj_kernel,
        grid=(t // tm, n // tn),
        in_specs=[
            pl.BlockSpec((tm, d), lambda i, j: (i, 0)),
            pl.BlockSpec((1, d), lambda i, j: (0, 0)),
            pl.BlockSpec((None, d, tn), lambda i, j: (j, 0, 0)),
            pl.BlockSpec((d, LANES), lambda i, j: (0, 0)),
        ],
        out_specs=[
            pl.BlockSpec((tm, tn), lambda i, j: (i, j)),
            pl.BlockSpec((tm, LANES), lambda i, j: (i, 0)),
        ],
        out_shape=[jax.ShapeDtypeStruct((t, n), BF16), jax.ShapeDtypeStruct((t, LANES), F32)],
        scratch_shapes=[pltpu.VMEM((tm, d), BF16)],
        compiler_params=_cparams("parallel", "arbitrary"),
        name="in_proj",
    )(x2d, gain, w_tiles, w_dt)


def _ssd_kernel(xbc_ref, z_ref, dt_ref, cw_ref, cb_ref, dtb_ref, alog_ref, dskip_ref, gain_ref, e_ref, shift_ref,
                o_ref, xh_ref, u_ref, state_ref, y_ref):
    q = SSD_CHUNK

    @pl.when(pl.program_id(1) == 0)
    def _():
        xh_ref[0:SSD_HISTORY_ROWS, :] = jnp.zeros((SSD_HISTORY_ROWS, xh_ref.shape[1]), BF16)
        state_ref[...] = jnp.zeros(state_ref.shape, F32)

    def chunk_body(sub, carry):
        rows = pl.ds(pl.multiple_of(sub * q, q), q)
        _ssd_chunk(xbc_ref.at[rows, :], z_ref.at[rows, :], dt_ref.at[rows, :], cw_ref, cb_ref, dtb_ref, alog_ref,
                   dskip_ref, gain_ref, e_ref, shift_ref, o_ref.at[rows, :], xh_ref, u_ref, state_ref, y_ref)
        return carry

    lax.fori_loop(0, xbc_ref.shape[0] // q, chunk_body, 0)


def _ssd_chunk(xbc_ref, z_ref, dt_ref, cw_ref, cb_ref, dtb_ref, alog_ref, dskip_ref, gain_ref, e_ref, shift_ref,
               o_ref, xh_ref, u_ref, state_ref, y_ref):
    q = SSD_CHUNK
    d_inner = SSD_GROUPS * SSD_GROUP_WIDTH
    gn = SSD_GROUPS * SSD_STATE
    hist = SSD_HISTORY_ROWS

    xh_ref[hist:hist + q, :] = xbc_ref[...]
    for cblk in range(xh_ref.shape[1] // SSD_CONV_COLS):
        cs = slice(cblk * SSD_CONV_COLS, (cblk + 1) * SSD_CONV_COLS)
        delayed = _dot(shift_ref[...], xh_ref[:, cs])
        acc = cb_ref[:, cs] + cw_ref[3:4, cs] * xbc_ref[:, cs].astype(F32)
        for k in range(1, SSD_CONV):
            acc = acc + cw_ref[SSD_CONV - 1 - k:SSD_CONV - k, cs] * delayed[(k - 1) * q:k * q, :]
        u_ref[:, cs] = _silu(acc)
    xh_ref[0:hist, :] = xh_ref[q:q + hist, :]

    pre = dt_ref[...] + dtb_ref[...]
    dt = jnp.maximum(pre, 0.0) + jnp.log1p(jnp.exp(-jnp.abs(pre)))
    adt = dt * (-jnp.exp(alog_ref[...]))
    row = lax.broadcasted_iota(jnp.int32, (q, q), 0)
    col = lax.broadcasted_iota(jnp.int32, (q, q), 1)
    causal = row >= col
    tril = jnp.where(causal, 1.0, 0.0).astype(BF16)
    a_cs = _dot_sel_left(tril, adt)
    a_cs_t = a_cs.T
    dt_x = _dot(dt.astype(BF16), e_ref[...])
    acs_x = _dot_sel_right(a_cs, e_ref[...])
    ecs_x = jnp.exp(acs_x)
    dstate_x = jnp.exp(acs_x[q - 1:q, :] - acs_x)
    chunk_decay_x = ecs_x[q - 1:q, :]

    cbs, xds = [], []
    for g in range(SSD_GROUPS):
        c0 = g * SSD_GROUP_WIDTH
        xs_g = u_ref[:, c0:c0 + SSD_GROUP_WIDTH]
        b_g = u_ref[:, d_inner + g * SSD_STATE:d_inner + (g + 1) * SSD_STATE]
        c_g = u_ref[:, d_inner + gn + g * SSD_STATE:d_inner + gn + (g + 1) * SSD_STATE].astype(BF16)
        b_gt = b_g.T.astype(BF16)
        cbs.append(_dot(c_g, b_gt))
        xd = xs_g * dt_x[:, c0:c0 + SSD_GROUP_WIDTH]
        xds.append(xd.astype(BF16))
        xs_b = (xd * dstate_x[:, c0:c0 + SSD_GROUP_WIDTH]).astype(BF16)
        new_state = _dot(b_gt, xs_b)
        s_in = state_ref[g]
        y_off = _dot(c_g, s_in.astype(BF16)) * ecs_x[:, c0:c0 + SSD_GROUP_WIDTH]
        state_ref[g] = s_in * chunk_decay_x[:, c0:c0 + SSD_GROUP_WIDTH] + new_state
        y_ref[:, c0:c0 + SSD_GROUP_WIDTH] = y_off + dskip_ref[:, c0:c0 + SSD_GROUP_WIDTH] * xs_g

    masked = []
    for hd in range(SSD_GROUPS * SSD_HEADS_PER_GROUP):
        seg = a_cs[:, hd:hd + 1] - a_cs_t[hd:hd + 1, :]
        decay = jnp.where(causal, jnp.exp(jnp.where(causal, seg, 0.0)), 0.0)
        masked.append((cbs[hd // SSD_HEADS_PER_GROUP] * decay).astype(BF16))

    lane = lax.broadcasted_iota(jnp.int32, (1, SSD_GROUP_WIDTH), 1)
    head_ind = [jnp.where(lane // SSD_HEAD_DIM == r, 1.0, 0.0).astype(BF16) for r in range(SSD_HEADS_PER_GROUP)]
    for g in range(SSD_GROUPS):
        c0 = g * SSD_GROUP_WIDTH
        y_diag = None
        for r in range(SSD_HEADS_PER_GROUP):
            part = _dot(masked[g * SSD_HEADS_PER_GROUP + r], xds[g] * head_ind[r])
            y_diag = part if y_diag is None else y_diag + part
        y_ref[:, c0:c0 + SSD_GROUP_WIDTH] += y_diag

    yz = y_ref[...] * _silu(z_ref[...].astype(F32))
    ms = jnp.mean(yz * yz, axis=-1, keepdims=True)
    o_ref[...] = (yz * lax.rsqrt(ms + EPS) * gain_ref[...]).astype(o_ref.dtype)


def _ssd(proj, dt_raw, conv_w, conv_b, dt_bias, a_log, d_skip_x, gain, expand, bsz, seq):
    q = SSD_CHUNK
    rows = SSD_CHUNKS_PER_STEP * q
    nc = seq // rows
    d_inner = SSD_GROUPS * SSD_GROUP_WIDTH
    conv_dim = conv_w.shape[1]
    const = lambda b, c: (0, 0)
    hist = SSD_HISTORY_ROWS
    out_row = jnp.arange((SSD_CONV - 1) * q)[:, None]
    src_col = jnp.arange(hist + q)[None, :]
    shift = (src_col == hist + out_row % q - (out_row // q + 1)).astype(BF16)
    return pl.pallas_call(
        _ssd_kernel,
        grid=(bsz, nc),
        in_specs=[
            pl.BlockSpec((rows, conv_dim), lambda b, c: (b * nc + c, 0)),
            pl.BlockSpec((rows, d_inner), lambda b, c: (b * nc + c, conv_dim // d_inner)),
            pl.BlockSpec((rows, LANES), lambda b, c: (b * nc + c, 0)),
            pl.BlockSpec((SSD_CONV, conv_dim), const),
            pl.BlockSpec((1, conv_dim), const),
            pl.BlockSpec((1, LANES), const),
            pl.BlockSpec((1, LANES), const),
            pl.BlockSpec((1, d_inner), const),
            pl.BlockSpec((1, d_inner), const),
            pl.BlockSpec((LANES, d_inner), const),
            pl.BlockSpec(((SSD_CONV - 1) * q, hist + q), const),
        ],
        out_specs=pl.BlockSpec((rows, d_inner), lambda b, c: (b * nc + c, 0)),
        out_shape=jax.ShapeDtypeStruct((bsz * seq, d_inner), BF16),
        scratch_shapes=[
            pltpu.VMEM((hist + q, conv_dim), BF16),
            pltpu.VMEM((q, conv_dim), F32),
            pltpu.VMEM((SSD_GROUPS, SSD_STATE, SSD_GROUP_WIDTH), F32),
            pltpu.VMEM((q, d_inner), F32),
        ],
        compiler_params=_cparams("parallel", "arbitrary"),
        name="ssd",
    )(proj, proj, dt_raw, conv_w, conv_b, dt_bias, a_log, d_skip_x, gain, expand, shift)


def _dil_prep_kernel(in_ref, pos_ref, qg_ref, kg_ref, invf_ref, bd_ref, o_ref, cos_ref, sin_ref, tmp_ref):
    kind = pl.program_id(1)
    g = pl.program_id(2)
    rows = in_ref.shape[0]

    @pl.when(jnp.logical_and(kind == 0, g == 0))
    def _():
        ang = pos_ref[...].astype(F32) * invf_ref[...]
        lane = lax.broadcasted_iota(jnp.int32, (1, LANES), 1)
        first_half = (lane % DIL_HEAD_DIM) < (DIL_HEAD_DIM // 2)
        cos_ref[...] = jnp.cos(ang)
        sin_ref[...] = jnp.sin(ang) * jnp.where(first_half, -1.0, 1.0)

    @pl.when(kind < 2)
    def _():
        gain = jnp.where(kind == 0, qg_ref[...] * (1.0 / math.sqrt(DIL_HEAD_DIM)), kg_ref[...])
        lane = lax.broadcasted_iota(jnp.int32, (1, LANES), 1)
        first_half = (lane % DIL_HEAD_DIM) < (DIL_HEAD_DIM // 2)
        for c in range(DIL_GROUP_WIDTH // LANES):
            xc = in_ref[:, c * LANES:(c + 1) * LANES].astype(F32)
            ss = _dot((xc * xc).astype(BF16), bd_ref[...])
            xn = xc * lax.rsqrt(ss * (1.0 / DIL_HEAD_DIM) + EPS) * gain
            partner = jnp.where(first_half, pltpu.roll(xn, LANES - DIL_HEAD_DIM // 2, 1),
                                pltpu.roll(xn, DIL_HEAD_DIM // 2, 1))
            tmp_ref[c] = xn * cos_ref[...] + partner * sin_ref[...]

    @pl.when(kind == 2)
    def _():
        for c in range(DIL_GROUP_WIDTH // LANES):
            tmp_ref[c] = in_ref[:, c * LANES:(c + 1) * LANES].astype(F32)

    @pl.when(g == 0)
    def _():
        for c in range(DIL_GROUP_WIDTH // LANES):
            o_ref[:, c * LANES:(c + 1) * LANES] = tmp_ref[c].astype(o_ref.dtype)

    for gi in (1, 2):
        d = DIL_DILATIONS[gi]

        @pl.when(g == gi)
        def _(d=d):
            for unit in range(rows // DIL_BLOCK):
                s, rho = unit // d, unit % d
                for c in range(DIL_GROUP_WIDTH // LANES):
                    src = tmp_ref[c, pl.ds(s * DIL_BLOCK * d + rho, DIL_BLOCK, stride=d), :]
                    o_ref[unit * DIL_BLOCK:(unit + 1) * DIL_BLOCK, c * LANES:(c + 1) * LANES] = src.astype(o_ref.dtype)


def _dil_prep(proj, pos2d, q_gain, k_gain, inv_freq, blockdiag, first_col_block):
    t = proj.shape[0]
    rows = DIL_SUPER
    w = DIL_GROUP_WIDTH
    const = lambda i, k, g: (0, 0)
    return pl.pallas_call(
        _dil_prep_kernel,
        grid=(t // rows, 3, 3),
        in_specs=[
            pl.BlockSpec((rows, w), lambda i, k, g: (i, first_col_block + k * 3 + g)),
            pl.BlockSpec((rows, 1), lambda i, k, g: (i, 0)),
            pl.BlockSpec((1, LANES), const),
            pl.BlockSpec((1, LANES), const),
            pl.BlockSpec((1, LANES), const),
            pl.BlockSpec((LANES, LANES), const),
        ],
        out_specs=pl.BlockSpec((None, rows, w), lambda i, k, g: (k * 3 + g, i, 0)),
        out_shape=jax.ShapeDtypeStruct((9, t, w), BF16),
        scratch_shapes=[pltpu.VMEM((rows, LANES), F32), pltpu.VMEM((rows, LANES), F32),
                        pltpu.VMEM((w // LANES, rows, LANES), F32)],
        compiler_params=_cparams("parallel", "arbitrary", "arbitrary"),
        name="dil_prep",
    )(proj, pos2d, q_gain, k_gain, inv_freq, blockdiag)


def _dil_attn_kernel(q_ref, kc_ref, kp_ref, vc_ref, vp_ref, o_ref, acc_ref, m_ref, l_ref):
    n = pl.program_id(1)
    g = pl.program_id(2)
    rows = q_ref.shape[0]
    blk = DIL_BLOCK
    n_units = rows // blk
    first_gi = len(DIL_DILATIONS) - 1
    last_gi = 0

    qi = lax.broadcasted_iota(jnp.int32, (blk, blk), 0)
    kj = lax.broadcasted_iota(jnp.int32, (blk, blk), 1)
    mask_cur = kj <= qi
    mask_prev = kj >= qi
    first_head = lax.broadcasted_iota(jnp.int32, (blk, LANES), 1) < DIL_HEAD_DIM
    head_ind = (jnp.where(first_head, 1.0, 0.0).astype(BF16), jnp.where(first_head, 0.0, 1.0).astype(BF16))

    def run_group(gi):
        d = DIL_DILATIONS[gi]

        def unit_body(u, carry, *, from_prev_block):
            q_row = pl.multiple_of(u * blk, blk)
            c_row = q_row
            if from_prev_block:
                kp_src, vp_src = kp_ref, vp_ref
                p_row = pl.multiple_of(rows + u * blk - d * blk, blk)
                prev_bias = jnp.where(n > 0, 0.0, NEG)
            else:
                kp_src, vp_src = kc_ref, vc_ref
                p_row = pl.multiple_of(u * blk - d * blk, blk)
                prev_bias = 0.0
            pos_row = (u // d) * (blk * d) + (u % d)
            pos_rows = pl.ds(pos_row, blk, stride=d) if d > 1 else pl.ds(pl.multiple_of(u * blk, blk), blk)
            n_hp = DIL_GROUP_WIDTH // LANES
            old = [] if gi == first_gi else [(m_ref[hp, pos_rows, :], acc_ref[hp, pos_rows, :], l_ref[hp, pos_rows, :])
                                             for hp in range(n_hp)]
            n_heads = DIL_GROUP_WIDTH // DIL_HEAD_DIM
            head_lanes = [slice(h * DIL_HEAD_DIM, (h + 1) * DIL_HEAD_DIM) for h in range(n_heads)]
            scores = []
            for lanes in head_lanes:
                qh = q_ref[pl.ds(q_row, blk), lanes]
                s_cur = jnp.where(mask_cur, _dot_nt(qh, kc_ref[pl.ds(c_row, blk), lanes]), NEG)
                s_prev = jnp.where(mask_prev, _dot_nt(qh, kp_src[pl.ds(p_row, blk), lanes]), NEG) + prev_bias
                scores.append((s_cur, s_prev))
            probs = []
            for s_cur, s_prev in scores:
                m_u = jnp.max(jnp.maximum(s_cur, s_prev), axis=1, keepdims=True)
                probs.append((jnp.exp(s_cur - m_u).astype(BF16), jnp.exp(s_prev - m_u).astype(BF16), m_u))
            new = []
            for hp in range(n_hp):
                pair = slice(hp * LANES, (hp + 1) * LANES)
                v_cur = vc_ref[pl.ds(c_row, blk), pair]
                v_prev = vp_src[pl.ds(p_row, blk), pair]
                ol = None
                for hh, ind in enumerate(head_ind):
                    p_cur, p_prev, _ = probs[hp * 2 + hh]
                    rhs_cur = jnp.concatenate([v_cur * ind, ind], axis=1)
                    rhs_prev = jnp.concatenate([v_prev * ind, ind], axis=1)
                    part = _dot(p_cur, rhs_cur) + _dot(p_prev, rhs_prev)
                    ol = part if ol is None else ol + part
                m_pair = jnp.where(first_head, probs[hp * 2][2], probs[hp * 2 + 1][2])
                new.append((m_pair, ol[:, :LANES], ol[:, LANES:]))
            for hp in range(n_hp):
                m_u, o_u, l_u = new[hp]
                if gi == first_gi:
                    acc_ref[hp, pos_rows, :] = o_u
                    m_ref[hp, pos_rows, :] = m_u
                    l_ref[hp, pos_rows, :] = l_u
                    continue
                m_old, acc_old, l_old = old[hp]
                m_new = jnp.maximum(m_old, m_u)
                a_old = jnp.exp(m_old - m_new)
                a_new = jnp.exp(m_u - m_new)
                acc_new = acc_old * a_old + o_u * a_new
                l_new = l_old * a_old + l_u * a_new
                if gi == last_gi:
                    o_ref[pos_rows, hp * LANES:(hp + 1) * LANES] = (acc_new / l_new).astype(o_ref.dtype)
                else:
                    acc_ref[hp, pos_rows, :] = acc_new
                    l_ref[hp, pos_rows, :] = l_new
                    m_ref[hp, pos_rows, :] = m_new
            return carry

        lax.fori_loop(0, d, functools.partial(unit_body, from_prev_block=True), 0)
        if d < n_units:
            lax.fori_loop(d, n_units, functools.partial(unit_body, from_prev_block=False), 0)

    for gi in range(len(DIL_DILATIONS)):
        pl.when(g == first_gi - gi)(functools.partial(run_group, gi))


def _dil_attn(qkv, bsz, seq):
    rows = DIL_SUPER
    w = DIL_GROUP_WIDTH
    nsb = seq // rows
    last = len(DIL_DILATIONS) - 1
    cur = lambda off: (lambda b, n, g: (off + last - g, b * nsb + n, 0))
    prev = lambda off: (lambda b, n, g: (off + last - g, b * nsb + jnp.maximum(n - 1, 0), 0))
    return pl.pallas_call(
        _dil_attn_kernel,
        grid=(bsz, nsb, 3),
        in_specs=[
            pl.BlockSpec((None, rows, w), cur(0)),
            pl.BlockSpec((None, rows, w), cur(3)),
            pl.BlockSpec((None, rows, w), prev(3)),
            pl.BlockSpec((None, rows, w), cur(6)),
            pl.BlockSpec((None, rows, w), prev(6)),
        ],
        out_specs=pl.BlockSpec((rows, w), lambda b, n, g: (b * nsb + n, 0)),
        out_shape=jax.ShapeDtypeStruct((bsz * seq, w), BF16),
        scratch_shapes=[pltpu.VMEM((w // LANES, rows, LANES), F32), pltpu.VMEM((w // LANES, rows, LANES), F32),
                        pltpu.VMEM((w // LANES, rows, LANES), F32)],
        compiler_params=_cparams("parallel", "arbitrary", "arbitrary"),
        name="dil_attn",
    )(qkv, qkv, qkv, qkv, qkv)


def _mem_kv_kernel(mem_ref, g_ref, w_ref, kg_ref, k_ref, v_ref):
    x = mem_ref[...]
    ms = jnp.mean(x * x, axis=-1, keepdims=True)
    h = (x * lax.rsqrt(ms + EPS) * g_ref[...]).astype(BF16)
    kv = _dot(h, w_ref[...])
    width = MEM_HEADS * MEM_HEAD_DIM
    for hd in range(MEM_HEADS):
        sl = slice(hd * MEM_HEAD_DIM, (hd + 1) * MEM_HEAD_DIM)
        kh = kv[:, sl]
        ms = jnp.mean(kh * kh, axis=-1, keepdims=True)
        k_ref[:, sl] = (kh * lax.rsqrt(ms + EPS) * kg_ref[...]).astype(k_ref.dtype)
    v_ref[...] = kv[:, width:].astype(v_ref.dtype)


def _mem_kv(mem2d, gain, w_kv, k_gain, bsz, mem_len):
    d = mem2d.shape[1]
    width = MEM_HEADS * MEM_HEAD_DIM
    const = lambda b: (0, 0)
    return pl.pallas_call(
        _mem_kv_kernel,
        grid=(bsz,),
        in_specs=[
            pl.BlockSpec((mem_len, d), lambda b: (b, 0)),
            pl.BlockSpec((1, d), const),
            pl.BlockSpec((d, 2 * width), const),
            pl.BlockSpec((1, MEM_HEAD_DIM), const),
        ],
        out_specs=[pl.BlockSpec((mem_len, width), lambda b: (b, 0)),
                   pl.BlockSpec((mem_len, width), lambda b: (b, 0))],
        out_shape=[jax.ShapeDtypeStruct((bsz * mem_len, width), BF16)] * 2,
        compiler_params=_cparams("parallel"),
        name="mem_kv",
    )(mem2d, gain, w_kv, k_gain)


def _mem_attn_kernel(q_ref, k_ref, v_ref, qg_ref, o_ref):
    for hd in range(MEM_HEADS):
        sl = slice(hd * MEM_HEAD_DIM, (hd + 1) * MEM_HEAD_DIM)
        qh = q_ref[:, sl].astype(F32)
        ms = jnp.mean(qh * qh, axis=-1, keepdims=True)
        qn = (qh * lax.rsqrt(ms + EPS) * qg_ref[...]).astype(BF16)
        s = _dot_nt(qn, k_ref[:, sl]) * (1.0 / math.sqrt(MEM_HEAD_DIM))
        p = jnp.exp(s - jnp.max(s, axis=-1, keepdims=True))
        den = jnp.sum(p, axis=-1, keepdims=True)
        o_ref[:, sl] = (_dot(p.astype(BF16), v_ref[:, sl]) / den).astype(o_ref.dtype)


def _mem_attn(proj, k, v, q_gain, bsz, seq, mem_len, q_col_block, tq=1024):
    width = MEM_HEADS * MEM_HEAD_DIM
    nq = seq // tq
    return pl.pallas_call(
        _mem_attn_kernel,
        grid=(bsz, nq),
        in_specs=[
            pl.BlockSpec((tq, width), lambda b, i: (b * nq + i, q_col_block)),
            pl.BlockSpec((mem_len, width), lambda b, i: (b, 0)),
            pl.BlockSpec((mem_len, width), lambda b, i: (b, 0)),
            pl.BlockSpec((1, MEM_HEAD_DIM), lambda b, i: (0, 0)),
        ],
        out_specs=pl.BlockSpec((tq, width), lambda b, i: (b * nq + i, 0)),
        out_shape=jax.ShapeDtypeStruct((bsz * seq, width), BF16),
        compiler_params=_cparams("parallel", "parallel"),
        name="mem_attn",
    )(proj, k, v, q_gain)


def _combine_kernel(ys_ref, yd_ref, ym_ref, gl_ref, x_ref, ws_ref, wd_ref, wm_ref, wo_ref, o_ref):
    d = x_ref.shape[1]
    gates = _sigmoid(gl_ref[...].astype(F32))
    merged = gates[:, 0:d] * _dot(ys_ref[...], ws_ref[...])
    merged = merged + gates[:, d:2 * d] * _dot(yd_ref[...], wd_ref[...])
    merged = merged + gates[:, 2 * d:3 * d] * _dot(ym_ref[...], wm_ref[...])
    o_ref[...] = x_ref[...] + _dot(merged.astype(BF16), wo_ref[...])


def _combine(y_ssd, y_dil, y_mem, proj, x2d, w_ssd, w_dil, w_mem, w_out, gate_col_block, tm=512):
    t, d = x2d.shape
    const = lambda i: (0, 0)
    row = lambda i: (i, 0)
    return pl.pallas_call(
        _combine_kernel,
        grid=(t // tm,),
        in_specs=[
            pl.BlockSpec((tm, y_ssd.shape[1]), row),
            pl.BlockSpec((tm, y_dil.shape[1]), row),
            pl.BlockSpec((tm, y_mem.shape[1]), row),
            pl.BlockSpec((tm, 3 * d), lambda i: (i, gate_col_block)),
            pl.BlockSpec((tm, d), row),
            pl.BlockSpec(w_ssd.shape, const),
            pl.BlockSpec(w_dil.shape, const),
            pl.BlockSpec(w_mem.shape, const),
            pl.BlockSpec(w_out.shape, const),
        ],
        out_specs=pl.BlockSpec((tm, d), row),
        out_shape=jax.ShapeDtypeStruct((t, d), F32),
        compiler_params=_cparams("parallel"),
        name="combine",
    )(y_ssd, y_dil, y_mem, proj, x2d, w_ssd, w_dil, w_mem, w_out)


RANK_CODE = -(2.0 ** 100)


def _top_rows(work, out_ref, count):
    for r in range(count):
        m = jnp.max(work, axis=0, keepdims=True)
        out_ref[r:r + 1, :] = m
        work = jnp.where(work == m, (r + 1) * RANK_CODE, work)
    return work


def _gelu(x):
    return 0.5 * x * (1.0 + lax.erf(x * (1.0 / math.sqrt(2.0))))


def _peer_kernel(x_ref, gain_ref, wq_ref, k1_ref, k2_ref, u_ref, vt_ref, o_ref,
                 hn_ref, cnt_ref, rank2_ref, p1_ref, p2_ref, top1_ref, top2_ref, cand_ref, sc_ref,
                 g_ref, acc_ref):
    e = pl.program_id(1)
    n_e = pl.num_programs(1)
    tt = x_ref.shape[0]
    eb = u_ref.shape[0]
    k = PEER_TOPK
    nk = PEER_KEYS
    n_tc = tt // LANES
    n_grp = tt // PEER_GROUP_TOKENS
    n_i = eb // nk
    assert n_i % 8 == 0, "whole aligned 8-row groups of key-1 rows per expert block"

    @pl.when(e == 0)
    def _():
        x = x_ref[...]
        ms = jnp.mean(x * x, axis=-1, keepdims=True)
        hn = x * lax.rsqrt(ms + EPS) * gain_ref[...]
        hn_t = hn.T.astype(BF16)
        for grp in range(n_grp):
            hn_ref[grp] = hn_t[:, grp * PEER_GROUP_TOKENS:(grp + 1) * PEER_GROUP_TOKENS]
        acc_ref[...] = jnp.zeros(acc_ref.shape, F32)
        for h in range(PEER_HEADS):
            r0 = h * 2 * PEER_HALF
            q1 = _dot(wq_ref[r0:r0 + PEER_HALF, :], hn_t)
            q2 = _dot(wq_ref[r0 + PEER_HALF:r0 + 2 * PEER_HALF, :], hn_t)
            s1 = _dot_f32(k1_ref[...], q1)
            s2 = _dot_f32(k2_ref[...], q2)
            ranked1 = _top_rows(s1, top1_ref, k)
            ranked2 = _top_rows(s2, top2_ref, k)
            v1 = top1_ref[...]
            v2 = top2_ref[...]
            brow = lax.broadcasted_iota(jnp.int32, (k, 1), 0)
            groups = []
            for a in range(k // 2):
                nb = k // (a + 1)
                rows_b = 8 if nb <= 8 else k
                blk = v1[a:a + 1, :] + v2[0:rows_b, :]
                groups.append(jnp.where(brow[0:rows_b] < nb, blk, -jnp.inf))
            groups.append(v1[k // 2:k, :] + v2[0:1, :])
            cand = jnp.concatenate(groups, axis=0)
            cand_ref[0:cand.shape[0], :] = cand
            _top_rows(cand_ref[0:cand.shape[0], :], sc_ref, k)
            sc = sc_ref[...]
            tau = sc[k - 1:k, :]
            z = jnp.sum(jnp.exp(sc - sc[0:1, :]), axis=0, keepdims=True)
            p1 = jnp.exp(s1 - v1[0:1, :])
            p2 = jnp.exp(s2 - v2[0:1, :]) / z
            cnt = jnp.zeros_like(s1)
            for a in range(k):
                cnt_a = jnp.sum(jnp.where(v1[a:a + 1, :] + v2 >= tau, 1.0, 0.0), axis=0, keepdims=True)
                cnt = jnp.where(ranked1 == (a + 1) * RANK_CODE, cnt_a, cnt)
            rank2 = jnp.where(ranked2 <= RANK_CODE, ranked2 * (1.0 / RANK_CODE) - 1.0, PEER_UNRANKED)
            for tc in range(n_tc):
                ts = slice(tc * LANES, (tc + 1) * LANES)
                cnt_ref[h, tc] = cnt[:, ts]
                p1_ref[h, tc] = p1[:, ts]
                rank2_ref[h, tc] = rank2[:, ts]
                p2_ref[h, tc] = p2[:, ts]

    i0 = pl.multiple_of(e * n_i, n_i)

    def gate_body(tc, carry):
        cnt_rows = [cnt_ref[h, tc, pl.ds(i0, n_i), :] for h in range(PEER_HEADS)]
        p1_rows = [p1_ref[h, tc, pl.ds(i0, n_i), :] for h in range(PEER_HEADS)]
        for ii in range(n_i):
            gsum = jnp.zeros((nk, LANES), F32)
            for h in range(PEER_HEADS):
                chosen = rank2_ref[h, tc] < cnt_rows[h][ii:ii + 1, :]
                gsum = gsum + jnp.where(chosen, p2_ref[h, tc] * p1_rows[h][ii:ii + 1, :], 0.0)
            g_ref[tc, ii * nk:(ii + 1) * nk, :] = gsum
        return carry

    lax.fori_loop(0, n_tc, gate_body, 0)

    hn_all = jnp.concatenate([hn_ref[grp] for grp in range(n_grp)], axis=1)
    act = _gelu(_dot(u_ref[...], hn_all))
    w = jnp.concatenate([(g_ref[tc] * act[:, tc * LANES:(tc + 1) * LANES]).astype(BF16) for tc in range(n_tc)],
                        axis=1)
    down = _dot(vt_ref[...], w)
    for grp in range(n_grp):
        acc_ref[grp] += down[:, grp * PEER_GROUP_TOKENS:(grp + 1) * PEER_GROUP_TOKENS]

    @pl.when(e == n_e - 1)
    def _():
        for grp in range(n_grp):
            cols = slice(grp * PEER_GROUP_TOKENS, (grp + 1) * PEER_GROUP_TOKENS)
            o_ref[cols, :] = x_ref[cols, :] + acc_ref[grp].T


def _transpose_cast_kernel(x_ref, o_ref):
    o_ref[...] = x_ref[...].T.astype(o_ref.dtype)


def _blocked_transpose(v, eb, tb=512):
    n_exp, d = v.shape
    per = eb // tb
    return pl.pallas_call(
        _transpose_cast_kernel,
        grid=(n_exp // tb,),
        in_specs=[pl.BlockSpec((tb, d), lambda r: (r, 0))],
        out_specs=pl.BlockSpec((None, d, tb), lambda r: (r // per, 0, r % per)),
        out_shape=jax.ShapeDtypeStruct((n_exp // eb, d, eb), BF16),
        compiler_params=_cparams("parallel"),
        name="peer_v_layout",
    )(v)


def _peer(x2d, gain, wq_t, keys1, keys2, u_b, v, tt=512, eb=2048):
    t, d = x2d.shape
    n_exp = u_b.shape[0]
    vt_b = _blocked_transpose(v, eb)
    const = lambda i, e: (0, 0)
    n_cand = 8 * 10
    n_tc = tt // LANES
    return pl.pallas_call(
        _peer_kernel,
        grid=(t // tt, n_exp // eb),
        in_specs=[
            pl.BlockSpec((tt, d), lambda i, e: (i, 0)),
            pl.BlockSpec((1, d), const),
            pl.BlockSpec(wq_t.shape, const),
            pl.BlockSpec(keys1.shape, const),
            pl.BlockSpec(keys2.shape, const),
            pl.BlockSpec((eb, d), lambda i, e: (e, 0)),
            pl.BlockSpec((None, d, eb), lambda i, e: (e, 0, 0)),
        ],
        out_specs=pl.BlockSpec((tt, d), lambda i, e: (i, 0)),
        out_shape=jax.ShapeDtypeStruct((t, d), F32),
        scratch_shapes=[
            pltpu.VMEM((tt // PEER_GROUP_TOKENS, d, PEER_GROUP_TOKENS), BF16),
            pltpu.VMEM((PEER_HEADS, n_tc, PEER_KEYS, LANES), F32),
            pltpu.VMEM((PEER_HEADS, n_tc, PEER_KEYS, LANES), F32),
            pltpu.VMEM((PEER_HEADS, n_tc, PEER_KEYS, LANES), F32),
            pltpu.VMEM((PEER_HEADS, n_tc, PEER_KEYS, LANES), F32),
            pltpu.VMEM((PEER_TOPK, tt), F32),
            pltpu.VMEM((PEER_TOPK, tt), F32),
            pltpu.VMEM((n_cand, tt), F32),
            pltpu.VMEM((PEER_TOPK, tt), F32),
            pltpu.VMEM((n_tc, eb, LANES), F32),
            pltpu.VMEM((tt // PEER_GROUP_TOKENS, d, PEER_GROUP_TOKENS), F32),
        ],
        compiler_params=_cparams("parallel", "arbitrary"),
        name="peer",
    )(x2d, gain, wq_t, keys1, keys2, u_b, vt_b)


def _pad_lanes(v, fill=0.0):
    return jnp.pad(v, ((0, 0), (0, LANES - v.shape[1])), constant_values=fill)


def _layer(x2d, mem2d, pos2d, bsz, seq, mem_len, norm_mix, w_in, conv_w, conv_b, dt_bias, a_log, d_skip,
           ssd_norm, dil_q_norm, dil_k_norm, mem_norm, w_mem_kv, mem_q_norm, mem_k_norm,
           w_up_ssd, w_up_dil, w_up_mem, w_out, norm_ffn, peer_w_q, peer_keys1, peer_keys2, peer_u, peer_v):
    d_inner = SSD_GROUPS * SSD_GROUP_WIDTH
    conv_dim = d_inner + 2 * SSD_GROUPS * SSD_STATE
    n_heads = SSD_GROUPS * SSD_HEADS_PER_GROUP
    dil_w = 3 * DIL_GROUP_WIDTH
    mem_w = MEM_HEADS * MEM_HEAD_DIM
    d = x2d.shape[1]

    o = 0
    parts = {}
    for name, width in (("z", d_inner), ("xbc", conv_dim), ("dt", n_heads), ("q_d", dil_w), ("k_d", dil_w),
                        ("v_d", dil_w), ("q_m", mem_w), ("gates", 3 * d)):
        parts[name] = w_in[:, o:o + width]
        o += width
    w_main = jnp.concatenate([parts[k].astype(BF16) for k in ("xbc", "z", "q_d", "k_d", "v_d", "q_m", "gates")],
                             axis=1)
    w_dt = _pad_lanes(parts["dt"])
    z_off = conv_dim
    qd_off = z_off + d_inner
    qm_off = qd_off + 3 * dil_w
    gate_off = qm_off + mem_w

    proj, dt_raw = _in_proj(x2d, norm_mix[None, :], w_main, w_dt)

    heads = jnp.arange(LANES)[:, None]
    chans = jnp.arange(d_inner)[None, :]
    expand = (chans // SSD_HEAD_DIM == heads).astype(BF16)
    y_ssd = _ssd(proj, dt_raw, conv_w, conv_b[None, :], _pad_lanes(dt_bias[None, :]), _pad_lanes(a_log[None, :]),
                 jnp.repeat(d_skip, SSD_HEAD_DIM)[None, :], ssd_norm[None, :], expand, bsz, seq)

    half = DIL_HEAD_DIM // 2
    inv_freq = ROPE_THETA ** (-jnp.arange(half, dtype=F32) / half)
    inv_freq = jnp.tile(inv_freq, LANES // half)[None, :]
    lane = jnp.arange(LANES)
    blockdiag = (lane[:, None] // DIL_HEAD_DIM == lane[None, :] // DIL_HEAD_DIM).astype(BF16)
    qkv = _dil_prep(proj, pos2d, jnp.tile(dil_q_norm, LANES // DIL_HEAD_DIM)[None, :],
                    jnp.tile(dil_k_norm, LANES // DIL_HEAD_DIM)[None, :], inv_freq, blockdiag,
                    qd_off // DIL_GROUP_WIDTH)
    y_dil = _dil_attn(qkv, bsz, seq)

    k_mem, v_mem = _mem_kv(mem2d, mem_norm[None, :], w_mem_kv.astype(BF16), mem_k_norm[None, :], bsz, mem_len)
    y_mem = _mem_attn(proj, k_mem, v_mem, mem_q_norm[None, :], bsz, seq, mem_len, qm_off // mem_w)

    x_mid = _combine(y_ssd, y_dil, y_mem, proj, x2d, w_up_ssd.astype(BF16), w_up_dil.astype(BF16),
                     w_up_mem.astype(BF16), w_out.astype(BF16), gate_off // (3 * d))

    return _peer(x_mid, norm_ffn[None, :], peer_w_q.T.astype(BF16), peer_keys1, peer_keys2,
                 peer_u.astype(BF16), peer_v)


def kernel(x, mem, positions, norm_mix, w_in, conv_w, conv_b, dt_bias, a_log, d_skip, ssd_norm, dil_q_norm,
           dil_k_norm, mem_norm, w_mem_kv, mem_q_norm, mem_k_norm, w_up_ssd, w_up_dil, w_up_mem, w_out,
           norm_ffn, peer_w_q, peer_keys1, peer_keys2, peer_u, peer_v):
    bsz, seq, d = x.shape
    mem_len = mem.shape[1]
    x2d = x.reshape(bsz * seq, d)
    mem2d = mem.reshape(bsz * mem_len, d)
    pos2d = positions.reshape(bsz * seq, 1)
    for layer in range(norm_mix.shape[0]):
        x2d = _layer(x2d, mem2d, pos2d, bsz, seq, mem_len, norm_mix[layer], w_in[layer], conv_w[layer],
                     conv_b[layer], dt_bias[layer], a_log[layer], d_skip[layer], ssd_norm[layer],
                     dil_q_norm[layer], dil_k_norm[layer], mem_norm[layer], w_mem_kv[layer], mem_q_norm[layer],
                     mem_k_norm[layer], w_up_ssd[layer], w_up_dil[layer], w_up_mem[layer], w_out[layer],
                     norm_ffn[layer], peer_w_q[layer], peer_keys1[layer], peer_keys2[layer], peer_u[layer],
                     peer_v[layer])
    return x2d.reshape(bsz, seq, d)
```
